```python
import math
import jax, jax.numpy as jnp
from jax import lax
import numpy as np

D_MODEL = 1024
BATCH = 32
SEQ = 2048
DEPTH = 4
DEC_BATCH = 1
DEC_SEQ = 16384
PAST_LEN = 128

N_MIXERS = 2
N_ATTN = (DEPTH + 1) // 2
N_MLSTM = DEPTH // 2
ATTN_HEADS = 8
ATTN_HEAD_DIM = 64
Q_BLOCK = 128
MLSTM_HEADS = 8
MLSTM_DK = D_MODEL // MLSTM_HEADS // 2
MLSTM_DV = D_MODEL // MLSTM_HEADS
MLSTM_CHUNK = 128
MLSTM_PROJ = 2 * MLSTM_HEADS * MLSTM_DK + MLSTM_HEADS * MLSTM_DV + D_MODEL + 4 * MLSTM_HEADS
N_EXPERTS = 16
CAPACITY_FACTOR = 2
EXPERT_FF = 2048
N_MOD = 6
EPS = 1e-6

kernel_name = "hybrid_diffattn_mlstm_ec_moe_encoder"


def rms_norm(x):
    x32 = x.astype(jnp.float32)
    return x32 * lax.rsqrt(jnp.mean(x32 * x32, axis=-1, keepdims=True) + EPS)


def diff_attention(h, w_in, w_out, q_gain, k_gain, lam, subln, lambda_init):
    B, S, _ = h.shape
    H, DH = ATTN_HEADS, ATTN_HEAD_DIM
    q, k, v = jnp.split(h @ w_in, 3, axis=-1)
    q = rms_norm(q.reshape(B, S, H, 2, DH)) * q_gain.astype(jnp.float32) * (DH ** -0.5)
    k = rms_norm(k.reshape(B, S, H, 2, DH)) * k_gain.astype(jnp.float32)
    v = v.reshape(B, S, H, 2 * DH).astype(jnp.float32)
    lam32 = lam.astype(jnp.float32)
    lmbda = (jnp.exp(jnp.sum(lam32[0] * lam32[1])) - jnp.exp(jnp.sum(lam32[2] * lam32[3]))
             + lambda_init)
    slopes = jnp.exp2(-8.0 * jnp.arange(1, H + 1, dtype=jnp.float32) / H)
    nb = S // Q_BLOCK
    qb = q.reshape(B, nb, Q_BLOCK, H, 2, DH).swapaxes(0, 1)
    pos_k = jnp.arange(S, dtype=jnp.float32)

    def block(args):
        q_blk, blk_idx = args
        pos_q = (blk_idx * Q_BLOCK).astype(jnp.float32) + jnp.arange(Q_BLOCK, dtype=jnp.float32)
        bias = -slopes[:, None, None] * jnp.abs(pos_q[:, None] - pos_k[None, :])
        s = jnp.einsum('bqhmd,bkhmd->bhmqk', q_blk, k) + bias[None, :, None]
        p = jax.nn.softmax(s, axis=-1)
        a = p[:, :, 0] - lmbda * p[:, :, 1]
        return jnp.einsum('bhqk,bkhe->bqhe', a, v)

    o = lax.map(block, (qb, jnp.arange(nb, dtype=jnp.int32)))
    o = o.swapaxes(0, 1).reshape(B, S, H, 2 * DH)
    o = rms_norm(o) * subln.astype(jnp.float32) * (1.0 - lambda_init)
    return o.reshape(B, S, H * 2 * DH).astype(h.dtype) @ w_out


def mlstm_chunk_scan(q, k, v, ig, lf):
    B, S, H, DK = q.shape
    DV = v.shape[-1]
    L = MLSTM_CHUNK
    nc = S // L

    def to_chunks(t):
        t = jnp.moveaxis(t, 2, 1)
        t = t.reshape((B, H, nc, L) + t.shape[3:])
        return jnp.moveaxis(t, 2, 0)

    causal = jnp.tril(jnp.ones((L, L), dtype=bool))

    def step(carry, xs):
        C, n, m = carry
        qc, kc, vc, ic, fc = xs
        b = jnp.cumsum(fc, axis=-1)
        g = b[..., -1]
        D = jnp.where(causal, b[..., :, None] - b[..., None, :] + ic[..., None, :], -jnp.inf)
        inter = b + m[..., None]
        m_t = jnp.maximum(inter, jnp.max(D, axis=-1))
        w = jnp.exp(D - m_t[..., None]) * jnp.einsum('bhtk,bhsk->bhts', qc, kc)
        s_inter = jnp.exp(inter - m_t)
        num = (s_inter[..., None] * jnp.einsum('bhtk,bhkv->bhtv', qc, C)
               + jnp.einsum('bhts,bhsv->bhtv', w, vc))
        den = s_inter * jnp.einsum('bhtk,bhk->bht', qc, n) + jnp.sum(w, axis=-1)
        h = num / jnp.maximum(jnp.abs(den), jnp.exp(-m_t))[..., None]
        a = g[..., None] - b + ic
        m_new = jnp.maximum(g + m, jnp.max(a, axis=-1))
        wk = jnp.exp(a - m_new[..., None])
        decay = jnp.exp(g + m - m_new)
        C_new = decay[..., None, None] * C + jnp.einsum('bhs,bhsk,bhsv->bhkv', wk, kc, vc)
        n_new = decay[..., None] * n + jnp.einsum('bhs,bhsk->bhk', wk, kc)
        return (C_new, n_new, m_new), h

    init = (jnp.zeros((B, H, DK, DV), jnp.float32), jnp.zeros((B, H, DK), jnp.float32),
            jnp.zeros((B, H), jnp.float32))
    _, hs = lax.scan(step, init, (to_chunks(q), to_chunks(k), to_chunks(v),
                                  to_chunks(ig), to_chunks(lf)))
    hs = jnp.moveaxis(hs, 0, 2).reshape(B, H, S, DV)
    return jnp.moveaxis(hs, 1, 2)


def bidirectional_mlstm(h, w_in, b_gate, out_norm, w_out):
    B, S, _ = h.shape
    H, DK, DV = MLSTM_HEADS, MLSTM_DK, MLSTM_DV
    proj = h @ w_in
    sizes = [H * DK, H * DK, H * DV, D_MODEL]
    offs = np.cumsum(sizes).tolist()
    q = proj[..., :offs[0]].reshape(B, S, H, DK).astype(jnp.float32)
    k = proj[..., offs[0]:offs[1]].reshape(B, S, H, DK).astype(jnp.float32) * (DK ** -0.5)
    v = proj[..., offs[1]:offs[2]].reshape(B, S, H, DV).astype(jnp.float32)
    o = proj[..., offs[2]:offs[3]]
    gates = (proj[..., offs[3]:] + b_gate).astype(jnp.float32).reshape(B, S, 4, H)
    i_fw, f_fw, i_bw, f_bw = gates[:, :, 0], gates[:, :, 1], gates[:, :, 2], gates[:, :, 3]
    h_fw = mlstm_chunk_scan(q, k, v, i_fw, jax.nn.log_sigmoid(f_fw))
    flip = lambda t: jnp.flip(t, axis=1)
    h_bw = flip(mlstm_chunk_scan(flip(q), flip(k), flip(v), flip(i_bw),
                                 flip(jax.nn.log_sigmoid(f_bw))))
    hn = rms_norm(h_fw + h_bw) * out_norm.astype(jnp.float32)
    hn = hn.reshape(B, S, H * DV) * jax.nn.sigmoid(o.astype(jnp.float32))
    return hn.astype(h.dtype) @ w_out


def expert_choice_moe(h, w_router, w_gate, w_up, w_down):
    B, S, D = h.shape
    N = B * S
    cap = (CAPACITY_FACTOR * N) // N_EXPERTS
    xt = h.reshape(N, D)
    aff = jax.nn.softmax((xt @ w_router).astype(jnp.float32), axis=-1)
    g, idx = lax.top_k(aff.T, cap)
    xe = xt[idx]
    hid = (jax.nn.silu(jnp.einsum('ecd,edf->ecf', xe, w_gate))
           * jnp.einsum('ecd,edf->ecf', xe, w_up))
    ye = jnp.einsum('ecf,efd->ecd', hid, w_down) * g[..., None].astype(h.dtype)
    out = jnp.zeros((N, D), h.dtype).at[idx.reshape(-1)].add(ye.reshape(-1, D))
    return out.reshape(B, S, D)


def trunk(x, c, norm_g, w_ada, b_ada, attn_w_in, attn_q_gain, attn_k_gain, attn_lambda,
          attn_subln, attn_w_out, mlstm_w_in, mlstm_b_gate, mlstm_out_norm, mlstm_w_out,
          w_router, w_exp_gate, w_exp_up, w_exp_down):
    B = x.shape[0]
    for l in range(DEPTH):
        mod = (jax.nn.silu(c) @ w_ada[l] + b_ada[l]).reshape(B, N_MOD, D_MODEL)[:, :, None, :]
        sh1, sc1, g1, sh2, sc2, g2 = [mod[:, j] for j in range(N_MOD)]
        h = (rms_norm(x) * norm_g[l, 0] * (1.0 + sc1) + sh1).astype(x.dtype)
        j = l // N_MIXERS
        if l % N_MIXERS == 0:
            lambda_init = 0.8 - 0.6 * math.exp(-0.3 * l)
            mix = diff_attention(h, attn_w_in[j], attn_w_out[j], attn_q_gain[j], attn_k_gain[j],
                                 attn_lambda[j], attn_subln[j], lambda_init)
        else:
            mix = bidirectional_mlstm(h, mlstm_w_in[j], mlstm_b_gate[j], mlstm_out_norm[j],
                                      mlstm_w_out[j])
        x = (x + g1 * mix).astype(x.dtype)
        h = (rms_norm(x) * norm_g[l, 1] * (1.0 + sc2) + sh2).astype(x.dtype)
        ff = expert_choice_moe(h, w_router[l], w_exp_gate[l], w_exp_up[l], w_exp_down[l])
        x = (x + g2 * ff).astype(x.dtype)
    return x


def setup_inputs(seed: int = 0) -> dict:
    key = jax.random.key(seed)
    ks = jax.random.split(key, 20)
    nrm = lambda k, shape, scale: jax.random.normal(k, shape, jnp.float32) * scale
    D, H = D_MODEL, MLSTM_HEADS
    kb1, kb2 = jax.random.split(ks[14])
    i_bias = nrm(kb1, (N_MLSTM, 2, H), 0.1)
    f_bias = jnp.linspace(3.0, 6.0, H, dtype=jnp.float32) + nrm(kb2, (N_MLSTM, 2, H), 0.1)
    mlstm_b_gate = jnp.stack([i_bias[:, 0], f_bias[:, 0], i_bias[:, 1], f_bias[:, 1]],
                             axis=1).reshape(N_MLSTM, 4 * H)
    ke = jax.random.split(ks[18], 3)
    return {
        "x_prompt": nrm(ks[0], (BATCH, SEQ, D), 1.0),
        "x_sample": nrm(ks[1], (DEC_BATCH, DEC_SEQ, D), 1.0),
        "c_prompt": nrm(ks[2], (BATCH, D), 1.0),
        "c_sample": nrm(ks[3], (DEC_BATCH, D), 1.0),
        "norm_g": 1.0 + nrm(ks[4], (DEPTH, 2, D), 0.02),
        "w_ada": nrm(ks[5], (DEPTH, D, N_MOD * D), 0.5 * D ** -0.5),
        "b_ada": nrm(ks[6], (DEPTH, N_MOD * D), 0.02),
        "attn_w_in": nrm(ks[7], (N_ATTN, D, 3 * ATTN_HEADS * 2 * ATTN_HEAD_DIM), D ** -0.5),
        "attn_q_gain": 1.0 + nrm(ks[8], (N_ATTN, ATTN_HEAD_DIM), 0.02),
        "attn_k_gain": 1.0 + nrm(ks[9], (N_ATTN, ATTN_HEAD_DIM), 0.02),
        "attn_lambda": nrm(ks[10], (N_ATTN, 4, ATTN_HEAD_DIM), 0.1),
        "attn_subln": 1.0 + nrm(ks[11], (N_ATTN, 2 * ATTN_HEAD_DIM), 0.02),
        "attn_w_out": nrm(ks[12], (N_ATTN, ATTN_HEADS * 2 * ATTN_HEAD_DIM, D), D ** -0.5),
        "mlstm_w_in": nrm(ks[13], (N_MLSTM, D, MLSTM_PROJ), D ** -0.5),
        "mlstm_b_gate": mlstm_b_gate,
        "mlstm_out_norm": 1.0 + nrm(ks[15], (N_MLSTM, MLSTM_DV), 0.02),
        "mlstm_w_out": nrm(ks[16], (N_MLSTM, MLSTM_HEADS * MLSTM_DV, D), D ** -0.5),
        "w_router": nrm(ks[17], (DEPTH, D, N_EXPERTS), D ** -0.5),
        "w_exp_gate": nrm(ke[0], (DEPTH, N_EXPERTS, D, EXPERT_FF), D ** -0.5),
        "w_exp_up": nrm(ke[1], (DEPTH, N_EXPERTS, D, EXPERT_FF), D ** -0.5),
        "w_exp_down": nrm(ke[2], (DEPTH, N_EXPERTS, EXPERT_FF, D), EXPERT_FF ** -0.5),
    }


def reference(x_prompt, x_sample, c_prompt, c_sample, norm_g, w_ada, b_ada, attn_w_in,
              attn_q_gain, attn_k_gain, attn_lambda, attn_subln, attn_w_out, mlstm_w_in,
              mlstm_b_gate, mlstm_out_norm, mlstm_w_out, w_router, w_exp_gate, w_exp_up,
              w_exp_down):
    y_prompt = trunk(x_prompt, c_prompt, norm_g, w_ada, b_ada, attn_w_in, attn_q_gain,
                     attn_k_gain, attn_lambda, attn_subln, attn_w_out, mlstm_w_in, mlstm_b_gate,
                     mlstm_out_norm, mlstm_w_out, w_router, w_exp_gate, w_exp_up, w_exp_down)
    y_sample = trunk(x_sample, c_sample, norm_g, w_ada, b_ada, attn_w_in, attn_q_gain,
                     attn_k_gain, attn_lambda, attn_subln, attn_w_out, mlstm_w_in, mlstm_b_gate,
                     mlstm_out_norm, mlstm_w_out, w_router, w_exp_gate, w_exp_up, w_exp_down)
    return (y_prompt, y_sample)
```

```python
import functools
import math

import jax
import jax.numpy as jnp
import numpy as np
from jax import lax
from jax.experimental import pallas as pl
from jax.experimental.pallas import tpu as pltpu

D_MODEL = 1024
DEPTH = 4
ATTN_HEADS = 8
ATTN_HEAD_DIM = 64
MLSTM_HEADS = 8
MLSTM_DK = 64
MLSTM_DV = 128
MLSTM_CHUNK = 128
N_EXPERTS = 16
CAPACITY_FACTOR = 2
EXPERT_FF = 2048
N_MOD = 6
EPS = 1e-6

LANES = 128
VMEM_LIMIT = 56 * 1024 * 1024

BF = jnp.bfloat16
F32 = jnp.float32
NEG = -1e30


def _cp(sem, vmem=VMEM_LIMIT):
    return pltpu.CompilerParams(dimension_semantics=sem, vmem_limit_bytes=vmem)


def _sigmoid(x):
    return 1.0 / (1.0 + jnp.exp(-x))


def _dot(a, b):
    return jnp.dot(a, b, preferred_element_type=F32)


def _dot_nt(a, b):
    return lax.dot_general(a, b, (((1,), (1,)), ((), ())), preferred_element_type=F32)


def _dot_tn(a, b):
    return lax.dot_general(a, b, (((0,), (0,)), ((), ())), preferred_element_type=F32)


def _split3(x):
    a = x.astype(BF)
    r = x - a.astype(F32)
    b = r.astype(BF)
    c = (r - b.astype(F32)).astype(BF)
    return a, b, c


def _ada_kernel(c_ref, w_ref, b_ref, o_ref):
    c = c_ref[...]
    s = (c * _sigmoid(c)).astype(BF)
    o_ref[...] = _dot(s, w_ref[...].astype(BF)) + b_ref[...]


def _ada(c_all, w_ada, b_ada):
    bp = c_all.shape[0]
    n_out = N_MOD * D_MODEL
    tn = 1536
    return pl.pallas_call(
        _ada_kernel,
        grid=(DEPTH, n_out // tn),
        in_specs=[
            pl.BlockSpec((bp, D_MODEL), lambda l, j: (0, 0)),
            pl.BlockSpec((None, D_MODEL, tn), lambda l, j: (l, 0, j)),
            pl.BlockSpec((None, 1, tn), lambda l, j: (l, 0, j)),
        ],
        out_specs=pl.BlockSpec((None, bp, tn), lambda l, j: (l, 0, j)),
        out_shape=jax.ShapeDtypeStruct((DEPTH, bp, n_out), F32),
        compiler_params=_cp(("arbitrary", "arbitrary")),
        name="ada",
    )(c_all, w_ada, b_ada.reshape(DEPTH, 1, n_out))


def _norm_mod(x, g, sc, sh):
    r = lax.rsqrt(jnp.mean(x * x, axis=-1, keepdims=True) + EPS)
    return x * r * g * (1.0 + sc) + sh


def _group_sumsq(y, bd):
    return _dot((y * y).astype(BF), bd)


def _block_diag_ones(group, size=256):
    i = np.arange(size) // group
    return jnp.asarray((i[:, None] == i[None, :]).astype(np.float32), dtype=BF)


def _inproj_attn_kernel(x_ref, g_ref, sc_ref, sh_ref, w_ref, gain_ref, bd_ref, o_ref):
    h = _norm_mod(x_ref[...], g_ref[...], sc_ref[...], sh_ref[...]).astype(BF)
    n_chunks = o_ref.shape[1] // 256
    n_norm = gain_ref.shape[1] // 256
    for j in range(n_chunks):
        sl = slice(j * 256, (j + 1) * 256)
        y = _dot(h, w_ref[:, sl])
        if j < n_norm:
            ss = _group_sumsq(y, bd_ref[...])
            y = y * lax.rsqrt(ss * (1.0 / ATTN_HEAD_DIM) + EPS) * gain_ref[:, sl]
        o_ref[:, sl] = y.astype(o_ref.dtype)


def _inproj_attn(x2, S, g, sc, sh, w_bf, gain_row, tm=512):
    N = x2.shape[0]
    n_out = w_bf.shape[1]
    spb = S // tm
    bd = _block_diag_ones(ATTN_HEAD_DIM)
    return pl.pallas_call(
        _inproj_attn_kernel,
        grid=(N // tm,),
        in_specs=[
            pl.BlockSpec((tm, D_MODEL), lambda i: (i, 0)),
            pl.BlockSpec((1, D_MODEL), lambda i: (0, 0)),
            pl.BlockSpec((None, 1, D_MODEL), lambda i: (i // spb, 0, 0)),
            pl.BlockSpec((None, 1, D_MODEL), lambda i: (i // spb, 0, 0)),
            pl.BlockSpec((D_MODEL, n_out), lambda i: (0, 0)),
            pl.BlockSpec((1, gain_row.shape[1]), lambda i: (0, 0)),
            pl.BlockSpec((256, 256), lambda i: (0, 0)),
        ],
        out_specs=pl.BlockSpec((tm, n_out), lambda i: (i, 0)),
        out_shape=jax.ShapeDtypeStruct((N, n_out), BF),
        compiler_params=_cp(("arbitrary",)),
        name="inproj_attn",
    )(x2, g, sc, sh, w_bf, gain_row, bd)


def _inproj_mlstm_kernel(x_ref, g_ref, sc_ref, sh_ref, w_ref, wg_ref, bg_ref, o_ref, og_ref):
    h = _norm_mod(x_ref[...], g_ref[...], sc_ref[...], sh_ref[...]).astype(BF)
    hk = MLSTM_HEADS * MLSTM_DK
    for j in range(o_ref.shape[1] // 256):
        sl = slice(j * 256, (j + 1) * 256)
        y = _dot(h, w_ref[:, sl])
        if hk <= j * 256 < 2 * hk:
            y = y * (MLSTM_DK ** -0.5)
        o_ref[:, sl] = y.astype(o_ref.dtype)
    og_ref[...] = _dot(h, wg_ref[...]) + bg_ref[...]


def _inproj_mlstm(x2, S, g, sc, sh, w_bf, wg_bf, bg_row, tm=512):
    N = x2.shape[0]
    n_out = w_bf.shape[1]
    spb = S // tm
    return pl.pallas_call(
        _inproj_mlstm_kernel,
        grid=(N // tm,),
        in_specs=[
            pl.BlockSpec((tm, D_MODEL), lambda i: (i, 0)),
            pl.BlockSpec((1, D_MODEL), lambda i: (0, 0)),
            pl.BlockSpec((None, 1, D_MODEL), lambda i: (i // spb, 0, 0)),
            pl.BlockSpec((None, 1, D_MODEL), lambda i: (i // spb, 0, 0)),
            pl.BlockSpec((D_MODEL, n_out), lambda i: (0, 0)),
            pl.BlockSpec((D_MODEL, LANES), lambda i: (0, 0)),
            pl.BlockSpec((1, LANES), lambda i: (0, 0)),
        ],
        out_specs=[
            pl.BlockSpec((tm, n_out), lambda i: (i, 0)),
            pl.BlockSpec((tm, LANES), lambda i: (i, 0)),
        ],
        out_shape=[
            jax.ShapeDtypeStruct((N, n_out), BF),
            jax.ShapeDtypeStruct((N, LANES), F32),
        ],
        compiler_params=_cp(("arbitrary",)),
        name="inproj_mlstm",
    )(x2, g, sc, sh, w_bf, wg_bf, bg_row)


def _attn_kernel(slopes_ref, q_ref, k_ref, v_ref, lam_ref, subln_ref, o_ref,
                 qs_ref, m_ref, l_ref, acc_ref, *, tq, tk, lambda_init):
    h = pl.program_id(1)
    qi = pl.program_id(2)
    ki = pl.program_id(3)
    nk = pl.num_programs(3)

    @pl.when(ki == 0)
    def _():
        q = q_ref[...]
        lane = lax.broadcasted_iota(jnp.int32, q.shape, 1)
        zero = jnp.zeros_like(q)
        qs_ref[0:tq, :] = jnp.where(lane < ATTN_HEAD_DIM, q, zero)
        qs_ref[tq:2 * tq, :] = jnp.where(lane >= ATTN_HEAD_DIM, q, zero)
        m_ref[...] = jnp.full(m_ref.shape, NEG, F32)
        l_ref[...] = jnp.zeros(l_ref.shape, F32)
        acc_ref[...] = jnp.zeros(acc_ref.shape, F32)

    s = _dot_nt(qs_ref[...], k_ref[...])
    row = lax.broadcasted_iota(jnp.int32, (tq, tk), 0)
    col = lax.broadcasted_iota(jnp.int32, (tq, tk), 1)
    dist = jnp.abs(row - col + (qi * tq - ki * tk)).astype(F32)
    bias = dist * (-slopes_ref[h])
    s = s + jnp.concatenate([bias, bias], axis=0)
    m_prev = m_ref[...]
    m_new = jnp.maximum(m_prev, jnp.max(s, axis=1, keepdims=True))
    alpha = jnp.exp(m_prev - m_new)
    p = jnp.exp(s - m_new)
    l_ref[...] = alpha * l_ref[...] + jnp.sum(p, axis=1, keepdims=True)
    acc_ref[...] = alpha * acc_ref[...] + _dot(p.astype(BF), v_ref[...])
    m_ref[...] = m_new

    @pl.when(ki == nk - 1)
    def _():
        lam = lam_ref[...]
        l01 = jnp.sum(lam[0:1, :] * lam[1:2, :], axis=1, keepdims=True)
        l23 = jnp.sum(lam[2:3, :] * lam[3:4, :], axis=1, keepdims=True)
        lmbda = jnp.exp(l01) - jnp.exp(l23) + lambda_init
        o0 = acc_ref[0:tq, :] / l_ref[0:tq, :]
        o1 = acc_ref[tq:2 * tq, :] / l_ref[tq:2 * tq, :]
        o = o0 - lmbda * o1
        r = lax.rsqrt(jnp.mean(o * o, axis=-1, keepdims=True) + EPS)
        o_ref[...] = (o * r * subln_ref[...] * (1.0 - lambda_init)).astype(o_ref.dtype)


def _attention(qkv, B, S, slopes, lam, subln_row, lambda_init, tq=512, tk=512):
    N = qkv.shape[0]
    H = ATTN_HEADS
    nq, nk = S // tq, S // tk
    kern = functools.partial(_attn_kernel, tq=tq, tk=tk, lambda_init=lambda_init)
    grid_spec = pltpu.PrefetchScalarGridSpec(
        num_scalar_prefetch=1,
        grid=(B, H, nq, nk),
        in_specs=[
            pl.BlockSpec((tq, LANES), lambda b, h, qi, ki, sl: (b * nq + qi, h)),
            pl.BlockSpec((tk, LANES), lambda b, h, qi, ki, sl: (b * nk + ki, H + h)),
            pl.BlockSpec((tk, LANES), lambda b, h, qi, ki, sl: (b * nk + ki, 2 * H + h)),
            pl.BlockSpec(lam.shape, lambda b, h, qi, ki, sl: (0, 0)),
            pl.BlockSpec((1, LANES), lambda b, h, qi, ki, sl: (0, 0)),
        ],
        out_specs=pl.BlockSpec((tq, LANES), lambda b, h, qi, ki, sl: (b * nq + qi, h)),
        scratch_shapes=[
            pltpu.VMEM((2 * tq, LANES), BF),
            pltpu.VMEM((2 * tq, 1), F32),
            pltpu.VMEM((2 * tq, 1), F32),
            pltpu.VMEM((2 * tq, LANES), F32),
        ],
    )
    return pl.pallas_call(
        kern,
        grid_spec=grid_spec,
        out_shape=jax.ShapeDtypeStruct((N, D_MODEL), BF),
        compiler_params=_cp(("arbitrary", "arbitrary", "arbitrary", "arbitrary")),
        name="diff_attn",
    )(slopes, qkv, qkv, qkv, lam, subln_row)


def _log_sigmoid(x):
    return jnp.minimum(x, 0.0) - jnp.log(1.0 + jnp.exp(-jnp.abs(x)))


def _mlstm_kernel(q_ref, k_ref, v_ref, gate_ref, tri_ref, o_ref, state_ref, m_ref):
    d = pl.program_id(1)
    c = pl.program_id(2)
    L = MLSTM_CHUNK
    H = MLSTM_HEADS

    @pl.when(c == 0)
    def _():
        state_ref[...] = jnp.zeros(state_ref.shape, F32)
        m_ref[...] = jnp.zeros(m_ref.shape, F32)

    G = gate_ref[...]
    G = jnp.where(d == 0, G, pltpu.roll(G, LANES - 2 * H, axis=1))
    lane = lax.broadcasted_iota(jnp.int32, G.shape, 1)
    A = jnp.where((lane >= H) & (lane < 2 * H), _log_sigmoid(G), G)
    AT = A.T
    tri_f = tri_ref[d]
    tri_d = tri_f.astype(BF)
    tri_o = tri_ref[1 - d].astype(BF)
    a1, a2, a3 = _split3(A)
    Bcol = _dot(tri_d, a1) + _dot(tri_d, a2) + _dot(tri_d, a3)
    t1, t2, t3 = _split3(AT)
    Brow = _dot(t1, tri_o) + _dot(t2, tri_o) + _dot(t3, tri_o)
    gtot = jnp.sum(A, axis=0, keepdims=True)
    mask = tri_f > 0.0

    kT =k_ref[...].astype(F32).T
    q = q_ref[...]
    v = v_ref[...]
    sub = lax.broadcasted_iota(jnp.int32, (2 * MLSTM_DK, L), 0)
    lane_v = lax.broadcasted_iota(jnp.int32, (L, LANES), 1)
    ones_col = jnp.where(lane_v == 0, 1.0, 0.0).astype(BF)

    for h in range(H):
        p, half = divmod(h, 2)
        ig_row = AT[h:h + 1, :]
        ig_col = A[:, h:h + 1]
        b_col = Bcol[:, H + h:H + h + 1]
        b_row = Brow[H + h:H + h + 1, :]
        g = gtot[:, H + h:H + h + 1]
        m = m_ref[h][:, 0:1]

        Dm = jnp.where(mask, b_col - b_row + ig_row, -jnp.inf)
        inter = b_col + m
        m_t = jnp.maximum(inter, jnp.max(Dm, axis=1, keepdims=True))

        q_pair = q[:, p * LANES:(p + 1) * LANES]
        own = (sub >= half * MLSTM_DK) & (sub < (half + 1) * MLSTM_DK)
        kT_h = jnp.where(own, kT[p * LANES:(p + 1) * LANES, :], 0.0)
        qk = _dot(q_pair, kT_h.astype(BF))
        w = jnp.exp(Dm - m_t) * qk
        v_ext = jnp.concatenate([v[:, h * LANES:(h + 1) * LANES], ones_col], axis=1)
        wv = _dot(w.astype(BF), v_ext)
        st = state_ref[h]
        qs = _dot(q_pair, st.astype(BF))
        s_inter = jnp.exp(inter - m_t)
        tot = s_inter * qs + wv
        num = tot[:, 0:LANES]
        den = tot[:, LANES:LANES + 1]
        hout = num / jnp.maximum(jnp.abs(den), jnp.exp(-m_t))
        o_ref[:, h * LANES:(h + 1) * LANES] = hout.astype(o_ref.dtype)

        a_row = g - b_row + ig_row
        m_new = jnp.maximum(g + m, jnp.max(a_row, axis=1, keepdims=True))
        wk_row = jnp.exp(a_row - m_new)
        decay = jnp.exp(g + m - m_new)
        kw = (kT_h * wk_row).astype(BF)
        state_ref[h] = decay * st + _dot(kw, v_ext)
        m_ref[h] = jnp.broadcast_to(m_new, (1, LANES))


def _mlstm(proj, gates, B, S):
    N = proj.shape[0]
    L = MLSTM_CHUNK
    nc = S // L
    H = MLSTM_HEADS
    t = np.arange(L)
    lower = (t[None, :] <= t[:, None]).astype(np.float32)
    tri = jnp.asarray(np.stack([lower, lower.T]), dtype=F32)

    def row(b, d, c):
        return b * nc + c + d * (nc - 1 - 2 * c)

    return pl.pallas_call(
        _mlstm_kernel,
        grid=(B, 2, nc),
        in_specs=[
            pl.BlockSpec((L, H * MLSTM_DK), lambda b, d, c: (row(b, d, c), 0)),
            pl.BlockSpec((L, H * MLSTM_DK), lambda b, d, c: (row(b, d, c), 1)),
            pl.BlockSpec((L, H * MLSTM_DV), lambda b, d, c: (row(b, d, c), 1)),
            pl.BlockSpec((L, LANES), lambda b, d, c: (row(b, d, c), 0)),
            pl.BlockSpec((2, L, L), lambda b, d, c: (0, 0, 0)),
        ],
        out_specs=pl.BlockSpec((None, L, H * MLSTM_DV), lambda b, d, c: (d, row(b, d, c), 0)),
        out_shape=jax.ShapeDtypeStruct((2, N, H * MLSTM_DV), BF),
        scratch_shapes=[
            pltpu.VMEM((H, 2 * MLSTM_DK, 2 * LANES), F32),
            pltpu.VMEM((H, 1, LANES), F32),
        ],
        compiler_params=_cp(("arbitrary", "arbitrary", "arbitrary")),
        name="mlstm",
    )(proj, proj, proj, gates, tri)


def _outproj_tail(a, x_ref, g1_ref, w_ref, n2_ref, sc2_ref, sh2_ref, wr_ref, xo_ref, h2_ref, lg_ref):
    xn = x_ref[...] + g1_ref[...] * _dot(a, w_ref[...])
    xo_ref[...] = xn
    h2 = _norm_mod(xn, n2_ref[...], sc2_ref[...], sh2_ref[...]).astype(BF)
    h2_ref[...] = h2
    lg = _dot_nt(wr_ref[...], h2)
    for cb in range(lg.shape[1] // LANES):
        lg_ref[cb] = lg[:, cb * LANES:(cb + 1) * LANES]


def _outproj_attn_kernel(a_ref, x_ref, g1_ref, w_ref, n2_ref, sc2_ref, sh2_ref, wr_ref,
                         xo_ref, h2_ref, lg_ref):
    _outproj_tail(a_ref[...], x_ref, g1_ref, w_ref, n2_ref, sc2_ref, sh2_ref, wr_ref,
                  xo_ref, h2_ref, lg_ref)


def _outproj_mlstm_kernel(hf_ref, hb_ref, og_ref, on_ref, bd_ref, x_ref, g1_ref, w_ref, n2_ref,
                          sc2_ref, sh2_ref, wr_ref, xo_ref, h2_ref, lg_ref, a_ref):
    for j in range(D_MODEL // 256):
        sl = slice(j * 256, (j + 1) * 256)
        hs = hf_ref[:, sl].astype(F32) + hb_ref[:, sl].astype(F32)
        ss = _group_sumsq(hs, bd_ref[...])
        hn = hs * lax.rsqrt(ss * (1.0 / MLSTM_DV) + EPS) * on_ref[:, sl]
        a_ref[:, sl] = (hn * _sigmoid(og_ref[:, sl].astype(F32))).astype(BF)
    _outproj_tail(a_ref[...], x_ref, g1_ref, w_ref, n2_ref, sc2_ref, sh2_ref, wr_ref,
                  xo_ref, h2_ref, lg_ref)


def _outproj(kind, mix_inputs, x2, S, g1, w_bf, n2, sc2, sh2, wrT_bf, tm=512):
    N = x2.shape[0]
    spb = S // tm
    row = lambda i: (i, 0)
    const = lambda i: (0, 0)
    per_b = lambda i: (i // spb, 0, 0)
    tail_specs = [
        pl.BlockSpec((tm, D_MODEL), row),
        pl.BlockSpec((None, 1, D_MODEL), per_b),
        pl.BlockSpec((D_MODEL, D_MODEL), const),
        pl.BlockSpec((1, D_MODEL), const),
        pl.BlockSpec((None, 1, D_MODEL), per_b),
        pl.BlockSpec((None, 1, D_MODEL), per_b),
        pl.BlockSpec((N_EXPERTS, D_MODEL), const),
    ]
    out_specs = [
        pl.BlockSpec((tm, D_MODEL), row),
        pl.BlockSpec((tm, D_MODEL), row),
        pl.BlockSpec((tm // LANES, N_EXPERTS, LANES), lambda i: (i, 0, 0)),
    ]
    out_shape = [
        jax.ShapeDtypeStruct((N, D_MODEL), F32),
        jax.ShapeDtypeStruct((N, D_MODEL), BF),
        jax.ShapeDtypeStruct((N // LANES, N_EXPERTS, LANES), F32),
    ]
    tail_args = (x2, g1, w_bf, n2, sc2, sh2, wrT_bf)
    if kind == "attn":
        (a,) = mix_inputs
        return pl.pallas_call(
            _outproj_attn_kernel,
            grid=(N // tm,),
            in_specs=[pl.BlockSpec((tm, D_MODEL), row)] + tail_specs,
            out_specs=out_specs, out_shape=out_shape,
            compiler_params=_cp(("arbitrary",)),
            name="outproj_attn",
        )(a, *tail_args)
    hfb, proj, onorm_row = mix_inputs
    bd = _block_diag_ones(MLSTM_DV)
    return pl.pallas_call(
        _outproj_mlstm_kernel,
        grid=(N // tm,),
        in_specs=[
            pl.BlockSpec((None, tm, D_MODEL), lambda i: (0, i, 0)),
            pl.BlockSpec((None, tm, D_MODEL), lambda i: (1, i, 0)),
            pl.BlockSpec((tm, D_MODEL), lambda i: (i, 2)),
            pl.BlockSpec((1, D_MODEL), const),
            pl.BlockSpec((256, 256), const),
        ] + tail_specs,
        out_specs=out_specs, out_shape=out_shape,
        scratch_shapes=[pltpu.VMEM((tm, D_MODEL), BF)],
        compiler_params=_cp(("arbitrary",)),
        name="outproj_mlstm",
    )(hfb, hfb, proj, onorm_row, bd, *tail_args)


def _router_kernel(lg_ref, triu_ref, rank_ref, gate_ref, offs_ref, bits_ref, *, cap, blocks_per_tb):
    NB = lg_ref.shape[0]
    E = N_EXPERTS
    shape = (E, LANES)

    def softmax_body(b, carry):
        l = lg_ref[b]
        e = jnp.exp(l - jnp.max(l, axis=0, keepdims=True))
        aff = e / jnp.sum(e, axis=0, keepdims=True)
        gate_ref[b] = aff
        bits_ref[b] = pltpu.bitcast(aff, jnp.int32)
        return carry

    lax.fori_loop(0, NB, softmax_body, 0)

    def count(pred):
        def body(b, acc):
            return acc + jnp.where(pred(bits_ref[b]), 1.0, 0.0)
        acc = lax.fori_loop(0, NB, body, jnp.zeros(shape, F32))
        return jnp.broadcast_to(jnp.sum(acc, axis=1, keepdims=True), shape)

    def search_body(i, T):
        cand = T | jnp.left_shift(jnp.int32(1), 30 - i)
        cnt = count(lambda x: x >= cand)
        return jnp.where(cnt >= cap, cand, T)

    T = lax.fori_loop(0, 31, search_body, jnp.zeros(shape, jnp.int32))
    need = cap - count(lambda x: x > T)

    triu = triu_ref[...]
    ones = jnp.ones((LANES, LANES), BF)

    def tb_body(tb, carry):
        ceq, csel = carry
        offs_ref[tb] = csel
        for k in range(blocks_per_tb):
            b = tb * blocks_per_tb + k
            x = bits_ref[b]
            gt = x > T
            eq = x == T
            eqf = jnp.where(eq, 1.0, 0.0)
            eqb = eqf.astype(BF)
            rank_eq = ceq + _dot(eqb, triu) - eqf
            sel = gt | (eq & (rank_eq < need))
            self_ = jnp.where(sel, 1.0, 0.0)
            selb = self_.astype(BF)
            rank = csel + _dot(selb, triu) - self_
            rank_ref[b] = jnp.where(sel, rank, -1.0).astype(jnp.int32)
            gate_ref[b] = jnp.where(sel, gate_ref[b], 0.0)
            ceq = ceq + _dot(eqb, ones)
            csel = csel + _dot(selb, ones)
        return ceq, csel

    n_tb = NB // blocks_per_tb
    _, csel = lax.fori_loop(0, n_tb, tb_body, (jnp.zeros(shape, F32), jnp.zeros(shape, F32)))
    offs_ref[n_tb] = csel


def _router(lg, cap, tb_tokens):
    NB = lg.shape[0]
    bpt = tb_tokens // LANES
    n_tb = NB // bpt
    u = np.arange(LANES)
    triu = jnp.asarray((u[:, None] <= u[None, :]).astype(np.float32), dtype=BF)
    kern = functools.partial(_router_kernel, cap=float(cap), blocks_per_tb=bpt)
    return pl.pallas_call(
        kern,
        out_shape=[
            jax.ShapeDtypeStruct((NB, N_EXPERTS, LANES), jnp.int32),
            jax.ShapeDtypeStruct((NB, N_EXPERTS, LANES), F32),
            jax.ShapeDtypeStruct((n_tb + 1, N_EXPERTS, LANES), F32),
        ],
        scratch_shapes=[pltpu.VMEM((NB, N_EXPERTS, LANES), jnp.int32)],
        compiler_params=pltpu.CompilerParams(vmem_limit_bytes=VMEM_LIMIT),
        name="router",
    )(lg, triu)


def _schedules(offs, cap, T, n_t):
    E = N_EXPERTS
    n_s = cap // T
    lo = offs[:-1]
    hi = offs[1:]
    nonempty = hi > lo
    s_lo = jnp.minimum(lo // T, n_s - 1)
    s_hi = jnp.where(nonempty, (hi - 1) // T, s_lo)
    cnt = jnp.where(nonempty, s_hi - s_lo + 1, 0)

    def build(cnt_flat, e_flat, t_flat, slo_flat, n_steps):
        ends = jnp.cumsum(cnt_flat)
        total = ends[-1]
        i = jnp.arange(n_steps, dtype=jnp.int32)
        ic = jnp.minimum(i, total - 1)
        pair = jnp.searchsorted(ends, ic, side="right").astype(jnp.int32)
        start = ends[pair] - cnt_flat[pair]
        s = slo_flat[pair] + (ic - start)
        return e_flat[pair], s.astype(jnp.int32), t_flat[pair], (i < total)

    e_idx = jnp.broadcast_to(jnp.arange(E, dtype=jnp.int32)[:, None], (E, n_t)).reshape(-1)
    t_idx = jnp.broadcast_to(jnp.arange(n_t, dtype=jnp.int32)[None, :], (E, n_t)).reshape(-1)
    n_f = E * (n_s + n_t)
    fe, fs, ft, fvalid = build(cnt.T.reshape(-1), e_idx, t_idx, s_lo.T.reshape(-1), n_f)
    key = fe * n_s + fs
    ffirst = fvalid & jnp.concatenate([jnp.ones((1,), bool), key[1:] != key[:-1]])
    flast = fvalid & jnp.concatenate([(key[1:] != key[:-1]) | ~fvalid[1:], jnp.ones((1,), bool)])
    fflags = (ffirst.astype(jnp.int32) + 2 * flast.astype(jnp.int32) + 4 * fvalid.astype(jnp.int32))

    cnt_c = cnt.at[:, 0].set(jnp.maximum(cnt[:, 0], 1))
    e_idx2 = jnp.broadcast_to(jnp.arange(E, dtype=jnp.int32)[None, :], (n_t, E)).reshape(-1)
    t_idx2 = jnp.broadcast_to(jnp.arange(n_t, dtype=jnp.int32)[:, None], (n_t, E)).reshape(-1)
    n_c = E * (n_s + n_t) + n_t
    ce, cs, ct, cvalid = build(cnt_c.reshape(-1), e_idx2, t_idx2, s_lo.reshape(-1), n_c)
    cfirst = cvalid & jnp.concatenate([jnp.ones((1,), bool), ct[1:] != ct[:-1]])
    clast = cvalid & jnp.concatenate([(ct[1:] != ct[:-1]) | ~cvalid[1:], jnp.ones((1,), bool)])
    cflags = (cfirst.astype(jnp.int32) + 2 * clast.astype(jnp.int32) + 4 * cvalid.astype(jnp.int32))
    return (fe, fs, ft, fflags), (ce, cs, ct, cflags)


def _one_hot_slots(rank_ref, e, s, T, n_blocks):
    slot = s * T + lax.broadcasted_iota(jnp.int32, (T, LANES), 0)
    pieces = [jnp.where(rank_ref[a, pl.ds(e, 1), :] == slot, 1.0, 0.0).astype(BF) for a in range(n_blocks)]
    return jnp.concatenate(pieces, axis=1)


def _ffn_kernel(se_ref, ss_ref, st_ref, fl_ref, x_ref, rank_ref, gate_ref, wg_ref, wu_ref, wd_ref,
                ye_ref, xacc_ref, gacc_ref, *, T, n_blocks, f_chunk):
    i = pl.program_id(0)
    e = se_ref[i]
    s = ss_ref[i]
    fl = fl_ref[i]

    @pl.when((fl & 1) != 0)
    def _():
        xacc_ref[...] = jnp.zeros(xacc_ref.shape, F32)
        gacc_ref[...] = jnp.zeros(gacc_ref.shape, F32)

    @pl.when((fl & 4) != 0)
    def _():
        P = _one_hot_slots(rank_ref, e, s, T, n_blocks)
        xacc_ref[...] += _dot(P, x_ref[...])
        g = jnp.concatenate([gate_ref[a, pl.ds(e, 1), :] for a in range(n_blocks)], axis=1)
        parts = [t.astype(F32) for t in _split3(g)]
        grows = jnp.concatenate(parts + [jnp.zeros((13, g.shape[1]), F32)], axis=0).astype(BF)
        gacc_ref[...] += _dot_nt(P, grows)

    @pl.when((fl & 2) != 0)
    def _():
        x = xacc_ref[...].astype(BF)
        y = jnp.zeros((T, D_MODEL), F32)
        for c in range(EXPERT_FF // f_chunk):
            sl = slice(c * f_chunk, (c + 1) * f_chunk)
            gt = _dot(x, wg_ref[:, sl])
            up = _dot(x, wu_ref[:, sl])
            hid = (gt * _sigmoid(gt) * up).astype(BF)
            y = y + _dot(hid, wd_ref[sl, :])
        ga = gacc_ref[...]
        gcol = ga[:, 0:1] + ga[:, 1:2] + ga[:, 2:3]
        ye_ref[...] = (y * gcol).astype(ye_ref.dtype)


def _ffn(sched, h2, rankm, gate, wg, wu, wd, cap, T, TB):
    se, ss, st, fl = sched
    n_steps = se.shape[0]
    nb = TB // LANES
    kern = functools.partial(_ffn_kernel, T=T, n_blocks=nb, f_chunk=512)
    grid_spec = pltpu.PrefetchScalarGridSpec(
        num_scalar_prefetch=4,
        grid=(n_steps,),
        in_specs=[
            pl.BlockSpec((TB, D_MODEL), lambda i, se, ss, st, fl: (st[i], 0)),
            pl.BlockSpec((nb, N_EXPERTS, LANES), lambda i, se, ss, st, fl: (st[i], 0, 0)),
            pl.BlockSpec((nb, N_EXPERTS, LANES), lambda i, se, ss, st, fl: (st[i], 0, 0)),
            pl.BlockSpec((None, D_MODEL, EXPERT_FF), lambda i, se, ss, st, fl: (se[i], 0, 0)),
            pl.BlockSpec((None, D_MODEL, EXPERT_FF), lambda i, se, ss, st, fl: (se[i], 0, 0)),
            pl.BlockSpec((None, EXPERT_FF, D_MODEL), lambda i, se, ss, st, fl: (se[i], 0, 0)),
        ],
        out_specs=pl.BlockSpec((None, T, D_MODEL), lambda i, se, ss, st, fl: (se[i], ss[i], 0)),
        scratch_shapes=[pltpu.VMEM((T, D_MODEL), F32), pltpu.VMEM((T, 16), F32)],
    )
    return pl.pallas_call(
        kern,
        grid_spec=grid_spec,
        out_shape=jax.ShapeDtypeStruct((N_EXPERTS, cap, D_MODEL), BF),
        compiler_params=_cp(("arbitrary",)),
        name="moe_ffn",
    )(se, ss, st, fl, h2, rankm, gate, wg, wu, wd)


def _combine_kernel(se_ref, ss_ref, st_ref, fl_ref, ye_ref, rank_ref, x_ref, g2_ref, o_ref, acc_ref,
                    *, T, n_blocks):
    i = pl.program_id(0)
    e = se_ref[i]
    s = ss_ref[i]
    fl = fl_ref[i]

    @pl.when((fl & 1) != 0)
    def _():
        acc_ref[...] = jnp.zeros(acc_ref.shape, F32)

    @pl.when((fl & 4) != 0)
    def _():
        P = _one_hot_slots(rank_ref, e, s, T, n_blocks)
        acc_ref[...] += _dot_tn(P, ye_ref[...])

    @pl.when((fl & 2) != 0)
    def _():
        o_ref[...] = x_ref[...] + g2_ref[...] * acc_ref[...]


def _combine(sched, ye, rankm, x1, g2, S, T, TB):
    se, ss, st, fl = sched
    n_steps = se.shape[0]
    N = x1.shape[0]
    nb = TB // LANES
    spb = S // TB
    kern = functools.partial(_combine_kernel, T=T, n_blocks=nb)
    grid_spec = pltpu.PrefetchScalarGridSpec(
        num_scalar_prefetch=4,
        grid=(n_steps,),
        in_specs=[
            pl.BlockSpec((None, T, D_MODEL), lambda i, se, ss, st, fl: (se[i], ss[i], 0)),
            pl.BlockSpec((nb, N_EXPERTS, LANES), lambda i, se, ss, st, fl: (st[i], 0, 0)),
            pl.BlockSpec((TB, D_MODEL), lambda i, se, ss, st, fl: (st[i], 0)),
            pl.BlockSpec((None, 1, D_MODEL), lambda i, se, ss, st, fl: (st[i] // spb, 0, 0)),
        ],
        out_specs=pl.BlockSpec((TB, D_MODEL), lambda i, se, ss, st, fl: (st[i], 0)),
        scratch_shapes=[pltpu.VMEM((TB, D_MODEL), F32)],
    )
    return pl.pallas_call(
        kern,
        grid_spec=grid_spec,
        out_shape=jax.ShapeDtypeStruct((N, D_MODEL), F32),
        compiler_params=_cp(("arbitrary",)),
        name="moe_combine",
    )(se, ss, st, fl, ye, rankm, x1, g2)


def _moe(h2, lg, x1, g2, S, wg, wu, wd, T, TB):
    N = x1.shape[0]
    cap = (CAPACITY_FACTOR * N) // N_EXPERTS
    T = min(T, cap)
    n_t = N // TB
    rankm, gate, offs = _router(lg, cap, TB)
    offs_i = offs[:, :, 0].astype(jnp.int32)
    fsched, csched = _schedules(offs_i, cap, T, n_t)
    ye = _ffn(fsched, h2, rankm, gate, wg, wu, wd, cap, T, TB)
    return _combine(csched, ye, rankm, x1, g2, S, T, TB)


def _trunk(x, mods, P, T=256, TB=512):
    B, S, _ = x.shape
    x2 = x.reshape(B * S, D_MODEL)
    for l in range(DEPTH):
        mod = mods[l]
        sh1, sc1, g1, sh2, sc2, g2 = [mod[:, j][:, None, :] for j in range(N_MOD)]
        n1 = P["norm_g"][l, 0][None, :]
        n2 = P["norm_g"][l, 1][None, :]
        j = l // 2
        if l % 2 == 0:
            lambda_init = 0.8 - 0.6 * math.exp(-0.3 * l)
            qkv = _inproj_attn(x2, S, n1, sc1, sh1, P["attn_w_in"][j], P["attn_gain"][j])
            a = _attention(qkv, B, S, P["slopes"], P["attn_lambda"][j], P["attn_subln"][j][None, :],
                           lambda_init, tq=min(512, S), tk=min(512, S))
            x1, h2, lg = _outproj("attn", (a,), x2, S, g1, P["attn_w_out"][j], n2, sc2, sh2,
                                  P["w_routerT"][l])
        else:
            proj, gates = _inproj_mlstm(x2, S, n1, sc1, sh1, P["mlstm_w_main"][j], P["mlstm_w_gate"][j],
                                        P["mlstm_b_gate"][j])
            hfb = _mlstm(proj, gates, B, S)
            x1, h2, lg = _outproj("mlstm", (hfb, proj, P["mlstm_out_norm"][j]), x2, S, g1,
                                  P["mlstm_w_out"][j], n2, sc2, sh2, P["w_routerT"][l])
        x2 = _moe(h2, lg, x1, g2, S, P["w_exp_gate"][l], P["w_exp_up"][l], P["w_exp_down"][l], T, TB)
    return x2.reshape(B, S, D_MODEL)


def _prepare(norm_g, attn_w_in, attn_q_gain, attn_k_gain, attn_lambda, attn_subln, attn_w_out,
             mlstm_w_in, mlstm_b_gate, mlstm_out_norm, mlstm_w_out, w_router, w_exp_gate, w_exp_up,
             w_exp_down):
    H = ATTN_HEADS
    n_main = 2 * MLSTM_HEADS * MLSTM_DK + MLSTM_HEADS * MLSTM_DV + D_MODEL
    n_gate = 4 * MLSTM_HEADS
    qg = jnp.tile(attn_q_gain * (ATTN_HEAD_DIM ** -0.5), (1, 2 * H))
    kg = jnp.tile(attn_k_gain, (1, 2 * H))
    return {
        "norm_g": norm_g,
        "attn_w_in": attn_w_in.astype(BF),
        "attn_gain": jnp.concatenate([qg, kg], axis=1)[:, None, :],
        "attn_lambda": attn_lambda,
        "attn_subln": attn_subln,
        "attn_w_out": attn_w_out.astype(BF),
        "slopes": jnp.exp2(-8.0 * jnp.arange(1, H + 1, dtype=F32) / H),
        "mlstm_w_main": mlstm_w_in[:, :, :n_main].astype(BF),
        "mlstm_w_gate": jnp.pad(mlstm_w_in[:, :, n_main:], ((0, 0), (0, 0), (0, LANES - n_gate))).astype(BF),
        "mlstm_b_gate": jnp.pad(mlstm_b_gate, ((0, 0), (0, LANES - n_gate)))[:, None, :],
        "mlstm_out_norm": jnp.tile(mlstm_out_norm, (1, MLSTM_HEADS))[:, None, :],
        "mlstm_w_out": mlstm_w_out.astype(BF),
        "w_routerT": jnp.swapaxes(w_router, 1, 2).astype(BF),
        "w_exp_gate": w_exp_gate.astype(BF),
        "w_exp_up": w_exp_up.astype(BF),
        "w_exp_down": w_exp_down.astype(BF),
    }


def kernel(x_prompt, x_sample, c_prompt, c_sample, norm_g, w_ada, b_ada, attn_w_in, attn_q_gain,
           attn_k_gain, attn_lambda, attn_subln, attn_w_out, mlstm_w_in, mlstm_b_gate, mlstm_out_norm,
           mlstm_w_out, w_router, w_exp_gate, w_exp_up, w_exp_down):
    P = _prepare(norm_g, attn_w_in, attn_q_gain, attn_k_gain, attn_lambda, attn_subln, attn_w_out,
                 mlstm_w_in, mlstm_b_gate, mlstm_out_norm, mlstm_w_out, w_router, w_exp_gate,
                 w_exp_up, w_exp_down)
    bp, bs = c_prompt.shape[0], c_sample.shape[0]
    pad = (-(bp + bs)) % 8
    c_all = jnp.concatenate([c_prompt, c_sample, jnp.zeros((pad, D_MODEL), F32)], axis=0)
    mods = _ada(c_all, w_ada, b_ada)
    mods = mods.reshape(DEPTH, c_all.shape[0], N_MOD, D_MODEL)
    y_prompt = _trunk(x_prompt, mods[:, :bp], P)
    y_sample = _trunk(x_sample, mods[:, bp:bp + bs], P)
    return (y_prompt, y_sample)
```

```python
import functools
import math

import jax
import jax.numpy as jnp
import numpy as np
from jax import lax
from jax.experimental import pallas as pl
from jax.experimental.pallas import tpu as pltpu

D_MODEL = 1024
DEPTH = 4
ATTN_HEADS = 8
ATTN_HEAD_DIM = 64
MLSTM_HEADS = 8
MLSTM_DK = 64
MLSTM_DV = 128
MLSTM_CHUNK = 128
N_EXPERTS = 16
CAPACITY_FACTOR = 2
EXPERT_FF = 2048
N_MOD = 6
EPS = 1e-6

LANES = 128
VMEM_LIMIT = 56 * 1024 * 1024

BF = jnp.bfloat16
F32 = jnp.float32
NEG = -1e30


def _cp(sem, vmem=VMEM_LIMIT):
    return pltpu.CompilerParams(dimension_semantics=sem, vmem_limit_bytes=vmem)


def _sigmoid(x):
    return 1.0 / (1.0 + jnp.exp(-x))


def _dot(a, b):
    return jnp.dot(a, b, preferred_element_type=F32)


def _dot_nt(a, b):
    return lax.dot_general(a, b, (((1,), (1,)), ((), ())), preferred_element_type=F32)


def _dot_tn(a, b):
    return lax.dot_general(a, b, (((0,), (0,)), ((), ())), preferred_element_type=F32)


def _split3(x):
    a = x.astype(BF)
    r = x - a.astype(F32)
    b = r.astype(BF)
    c = (r - b.astype(F32)).astype(BF)
    return a, b, c


def _ada_kernel(c_ref, w_ref, b_ref, o_ref):
    c = c_ref[...]
    s = (c * _sigmoid(c)).astype(BF)
    o_ref[...] = _dot(s, w_ref[...].astype(BF)) + b_ref[...]


def _ada(c_all, w_ada, b_ada):
    bp = c_all.shape[0]
    n_out = N_MOD * D_MODEL
    tn = 1536
    return pl.pallas_call(
        _ada_kernel,
        grid=(DEPTH, n_out // tn),
        in_specs=[
            pl.BlockSpec((bp, D_MODEL), lambda l, j: (0, 0)),
            pl.BlockSpec((None, D_MODEL, tn), lambda l, j: (l, 0, j)),
            pl.BlockSpec((None, 1, tn), lambda l, j: (l, 0, j)),
        ],
        out_specs=pl.BlockSpec((None, bp, tn), lambda l, j: (l, 0, j)),
        out_shape=jax.ShapeDtypeStruct((DEPTH, bp, n_out), F32),
        compiler_params=_cp(("arbitrary", "arbitrary")),
        name="ada",
    )(c_all, w_ada, b_ada.reshape(DEPTH, 1, n_out))


def _norm_mod(x, g, sc, sh):
    r = lax.rsqrt(jnp.mean(x * x, axis=-1, keepdims=True) + EPS)
    return x * r * g * (1.0 + sc) + sh


def _group_sumsq(y, bd):
    return _dot((y * y).astype(BF), bd)


def _block_diag_ones(group, size=256):
    i = np.arange(size) // group
    return jnp.asarray((i[:, None] == i[None, :]).astype(np.float32), dtype=BF)


def _inproj_attn_kernel(x_ref, g_ref, sc_ref, sh_ref, w_ref, gain_ref, bd_ref, o_ref):
    h = _norm_mod(x_ref[...], g_ref[...], sc_ref[...], sh_ref[...]).astype(BF)
    n_chunks = o_ref.shape[1] // 256
    n_norm = gain_ref.shape[1] // 256
    for j in range(n_chunks):
        sl = slice(j * 256, (j + 1) * 256)
        y = _dot(h, w_ref[:, sl])
        if j < n_norm:
            ss = _group_sumsq(y, bd_ref[...])
            y = y * lax.rsqrt(ss * (1.0 / ATTN_HEAD_DIM) + EPS) * gain_ref[:, sl]
        o_ref[:, sl] = y.astype(o_ref.dtype)


def _inproj_attn(x2, S, g, sc, sh, w_bf, gain_row, tm=512):
    N = x2.shape[0]
    n_out = w_bf.shape[1]
    spb = S // tm
    bd = _block_diag_ones(ATTN_HEAD_DIM)
    return pl.pallas_call(
        _inproj_attn_kernel,
        grid=(N // tm,),
        in_specs=[
            pl.BlockSpec((tm, D_MODEL), lambda i: (i, 0)),
            pl.BlockSpec((1, D_MODEL), lambda i: (0, 0)),
            pl.BlockSpec((None, 1, D_MODEL), lambda i: (i // spb, 0, 0)),
            pl.BlockSpec((None, 1, D_MODEL), lambda i: (i // spb, 0, 0)),
            pl.BlockSpec((D_MODEL, n_out), lambda i: (0, 0)),
            pl.BlockSpec((1, gain_row.shape[1]), lambda i: (0, 0)),
            pl.BlockSpec((256, 256), lambda i: (0, 0)),
        ],
        out_specs=pl.BlockSpec((tm, n_out), lambda i: (i, 0)),
        out_shape=jax.ShapeDtypeStruct((N, n_out), BF),
        compiler_params=_cp(("arbitrary",)),
        name="inproj_attn",
    )(x2, g, sc, sh, w_bf, gain_row, bd)


def _inproj_mlstm_kernel(x_ref, g_ref, sc_ref, sh_ref, w_ref, wg_ref, bg_ref, o_ref, og_ref):
    h = _norm_mod(x_ref[...], g_ref[...], sc_ref[...], sh_ref[...]).astype(BF)
    hk = MLSTM_HEADS * MLSTM_DK
    for j in range(o_ref.shape[1] // 256):
        sl = slice(j * 256, (j + 1) * 256)
        y = _dot(h, w_ref[:, sl])
        if hk <= j * 256 < 2 * hk:
            y = y * (MLSTM_DK ** -0.5)
        o_ref[:, sl] = y.astype(o_ref.dtype)
    og_ref[...] = _dot(h, wg_ref[...]) + bg_ref[...]


def _inproj_mlstm(x2, S, g, sc, sh, w_bf, wg_bf, bg_row, tm=512):
    N = x2.shape[0]
    n_out = w_bf.shape[1]
    spb = S // tm
    return pl.pallas_call(
        _inproj_mlstm_kernel,
        grid=(N // tm,),
        in_specs=[
            pl.BlockSpec((tm, D_MODEL), lambda i: (i, 0)),
            pl.BlockSpec((1, D_MODEL), lambda i: (0, 0)),
            pl.BlockSpec((None, 1, D_MODEL), lambda i: (i // spb, 0, 0)),
            pl.BlockSpec((None, 1, D_MODEL), lambda i: (i // spb, 0, 0)),
            pl.BlockSpec((D_MODEL, n_out), lambda i: (0, 0)),
            pl.BlockSpec((D_MODEL, LANES), lambda i: (0, 0)),
            pl.BlockSpec((1, LANES), lambda i: (0, 0)),
        ],
        out_specs=[
            pl.BlockSpec((tm, n_out), lambda i: (i, 0)),
            pl.BlockSpec((tm, LANES), lambda i: (i, 0)),
        ],
        out_shape=[
            jax.ShapeDtypeStruct((N, n_out), BF),
            jax.ShapeDtypeStruct((N, LANES), F32),
        ],
        compiler_params=_cp(("arbitrary",)),
        name="inproj_mlstm",
    )(x2, g, sc, sh, w_bf, wg_bf, bg_row)


_LOG2E = math.log2(math.e)
N_FEAT = 12


def _log2e_split():
    c = np.float32(_LOG2E)
    out = []
    for _ in range(3):
        t = np.asarray(c, dtype=BF).astype(np.float32)
        out.append(float(t))
        c = np.float32(c - t)
    return out


def _attn_tables(S):
    c = _log2e_split()
    pos = np.arange(S)
    kf = np.zeros((S, LANES), np.float32)
    qc = np.zeros((1, LANES), np.float32)
    for t in range(3):
        kf[:, t] = -128.0 * c[t]
        kf[:, 3 + t] = -c[t]
        kf[:, 6 + t] = pos // 128
        kf[:, 9 + t] = pos % 128
        qc[0, 6 + t] = 128.0 * c[t]
        qc[0, 9 + t] = c[t]
    return jnp.asarray(kf, dtype=BF), jnp.asarray(qc, dtype=F32)


def _attn_kernel(slopes_ref, dmax_ref, q_ref, k_ref, v_ref, kf_ref, qc_ref, lam_ref, subln_ref, o_ref,
                 qs_ref, m_ref, l_ref, acc_ref, *, tq, tk, S, lambda_init):
    h = pl.program_id(1)
    qi = pl.program_id(2)
    nk = S // tk
    slope = slopes_ref[h]
    r0 = qi * tq

    q = q_ref[...]
    lane = lax.broadcasted_iota(jnp.int32, (tq, LANES), 1)
    zero = jnp.zeros_like(q)
    q0 = jnp.where(lane < ATTN_HEAD_DIM, q, zero)
    q1 = jnp.where(lane >= ATTN_HEAD_DIM, q, zero)
    pos = r0 + lax.broadcasted_iota(jnp.int32, (tq, LANES), 0)
    hi = (pos >> 7).astype(F32)
    lo = (pos & 127).astype(F32)
    feat = jnp.where(lane < 3, hi, jnp.where(lane < 6, lo, qc_ref[...])) * slope
    for sgn, f in ((0, feat.astype(BF)), (1, (-feat).astype(BF))):
        qs_ref[sgn, 0:tq, 0:LANES] = q0
        qs_ref[sgn, tq:2 * tq, 0:LANES] = q1
        qs_ref[sgn, 0:tq, LANES:2 * LANES] = f
        qs_ref[sgn, tq:2 * tq, LANES:2 * LANES] = f

    def scores(ki, sgn):
        off = pl.multiple_of(ki * tk, tk)
        kaug = jnp.concatenate([k_ref[pl.ds(off, tk), :], kf_ref[pl.ds(off, tk), :]], axis=1)
        return _dot_nt(qs_ref[sgn], kaug), off

    def chunk(ki, sgn, state):
        m_prev, l_prev, acc_prev = state
        s, off = scores(ki, sgn)
        m_new = jnp.maximum(m_prev, jnp.max(s, axis=1, keepdims=True))
        p = jnp.exp2(s - pltpu.repeat(m_new, tk // LANES, axis=1))
        alpha = jnp.exp2(m_prev - m_new)
        l_new = alpha * l_prev + jnp.sum(p, axis=1, keepdims=True)
        acc_new = alpha * acc_prev + _dot(p.astype(BF), v_ref[pl.ds(off, tk), :])
        return m_new, l_new, acc_new

    kd = r0 // tk
    s, off = scores(kd, 0)
    row = lax.broadcasted_iota(jnp.int32, (tq, tk), 0)
    col = lax.broadcasted_iota(jnp.int32, (tq, tk), 1)
    corr = jnp.minimum(row - col + (r0 - off), 0).astype(F32) * (2.0 * _LOG2E * slope)
    s = s + jnp.concatenate([corr, corr], axis=0)
    m0 = jnp.max(s, axis=1, keepdims=True)
    p = jnp.exp2(s - m0)
    m_ref[...] = jnp.broadcast_to(m0, m_ref.shape)
    l_ref[...] = jnp.broadcast_to(jnp.sum(p, axis=1, keepdims=True), l_ref.shape)
    acc_ref[...] = _dot(p.astype(BF), v_ref[pl.ds(off, tk), :])

    dmax = dmax_ref[h]
    k_lo = jnp.maximum(r0 - dmax, 0) // tk
    k_hi = jnp.minimum((r0 + tq - 1 + dmax) // tk, nk - 1)
    n_before = kd - k_lo
    n_off = n_before + (k_hi - kd)

    def nth(i):
        after = (i >= n_before).astype(jnp.int32)
        return k_lo + i + after, after

    def load_state():
        return m_ref[...], l_ref[...], acc_ref[...]

    def store_state(state):
        m_ref[...], l_ref[...], acc_ref[...] = state

    def pair_body(j, c):
        state = load_state()
        state = chunk(*nth(2 * j), state)
        state = chunk(*nth(2 * j + 1), state)
        store_state(state)
        return c

    lax.fori_loop(0, n_off // 2, pair_body, 0)

    @pl.when(n_off % 2 == 1)
    def _():
        store_state(chunk(*nth(n_off - 1), load_state()))


    lam = lam_ref[...]
    l01 = jnp.sum(lam[0:1, :] * lam[1:2, :], axis=1, keepdims=True)
    l23 = jnp.sum(lam[2:3, :] * lam[3:4, :], axis=1, keepdims=True)
    lmbda = jnp.exp(l01) - jnp.exp(l23) + lambda_init
    o0 = acc_ref[0:tq, :] / l_ref[0:tq, :]
    o1 = acc_ref[tq:2 * tq, :] / l_ref[tq:2 * tq, :]
    o = o0 - lmbda * o1
    r = lax.rsqrt(jnp.mean(o * o, axis=-1, keepdims=True) + EPS)
    o_ref[...] = (o * r * subln_ref[...] * (1.0 - lambda_init)).astype(o_ref.dtype)


def _attention(qkv, B, S, slopes, dmax, lam, subln_row, lambda_init, tq=512, tk=512):
    N = qkv.shape[0]
    H = ATTN_HEADS
    tq, tk = min(tq, S), min(tk, S)
    assert tk % tq == 0 and S % tk == 0 and S <= 128 * 256
    nq = S // tq
    kf, qc = _attn_tables(S)
    kern = functools.partial(_attn_kernel, tq=tq, tk=tk, S=S, lambda_init=lambda_init)
    grid_spec = pltpu.PrefetchScalarGridSpec(
        num_scalar_prefetch=2,
        grid=(B, H, nq),
        in_specs=[
            pl.BlockSpec((tq, LANES), lambda b, h, qi, sl, dm: (b * nq + qi, h)),
            pl.BlockSpec((S, LANES), lambda b, h, qi, sl, dm: (b, H + h)),
            pl.BlockSpec((S, LANES), lambda b, h, qi, sl, dm: (b, 2 * H + h)),
            pl.BlockSpec((S, LANES), lambda b, h, qi, sl, dm: (0, 0)),
            pl.BlockSpec((1, LANES), lambda b, h, qi, sl, dm: (0, 0)),
            pl.BlockSpec(lam.shape, lambda b, h, qi, sl, dm: (0, 0)),
            pl.BlockSpec((1, LANES), lambda b, h, qi, sl, dm: (0, 0)),
        ],
        out_specs=pl.BlockSpec((tq, LANES), lambda b, h, qi, sl, dm: (b * nq + qi, h)),
        scratch_shapes=[
            pltpu.VMEM((2, 2 * tq, 2 * LANES), BF),
            pltpu.VMEM((2 * tq, LANES), F32),
            pltpu.VMEM((2 * tq, LANES), F32),
            pltpu.VMEM((2 * tq, LANES), F32),
        ],
    )
    return pl.pallas_call(
        kern,
        grid_spec=grid_spec,
        out_shape=jax.ShapeDtypeStruct((N, D_MODEL), BF),
        compiler_params=_cp(("arbitrary", "arbitrary", "arbitrary")),
        name="diff_attn",
    )(slopes, dmax, qkv, qkv, qkv, kf, qc, lam, subln_row)


def _log_sigmoid(x):
    return jnp.minimum(x, 0.0) - jnp.log(1.0 + jnp.exp(-jnp.abs(x)))


def _mlstm_kernel(q_ref, k_ref, v_ref, gate_ref, tri_ref, o_ref, state_ref, m_ref):
    d = pl.program_id(1)
    c = pl.program_id(2)
    L = MLSTM_CHUNK
    H = MLSTM_HEADS

    @pl.when(c == 0)
    def _():
        state_ref[...] = jnp.zeros(state_ref.shape, F32)
        m_ref[...] = jnp.zeros(m_ref.shape, F32)

    G = gate_ref[...]
    G = jnp.where(d == 0, G, pltpu.roll(G, LANES - 2 * H, axis=1))
    lane = lax.broadcasted_iota(jnp.int32, G.shape, 1)
    A = jnp.where((lane >= H) & (lane < 2 * H), _log_sigmoid(G), G)
    AT = A.T
    tri_f = tri_ref[d]
    tri_d = tri_f.astype(BF)
    tri_o = tri_ref[1 - d].astype(BF)
    a1, a2, a3 = _split3(A)
    Bcol = _dot(tri_d, a1) + _dot(tri_d, a2) + _dot(tri_d, a3)
    t1, t2, t3 = _split3(AT)
    Brow = _dot(t1, tri_o) + _dot(t2, tri_o) + _dot(t3, tri_o)
    gtot = jnp.sum(A, axis=0, keepdims=True)
    mask = tri_f > 0.0

    kT =k_ref[...].astype(F32).T
    q = q_ref[...]
    v = v_ref[...]
    sub = lax.broadcasted_iota(jnp.int32, (2 * MLSTM_DK, L), 0)
    lane_v = lax.broadcasted_iota(jnp.int32, (L, LANES), 1)
    ones_col = jnp.where(lane_v == 0, 1.0, 0.0).astype(BF)

    for h in range(H):
        p, half = divmod(h, 2)
        ig_row = AT[h:h + 1, :]
        ig_col = A[:, h:h + 1]
        b_col = Bcol[:, H + h:H + h + 1]
        b_row = Brow[H + h:H + h + 1, :]
        g = gtot[:, H + h:H + h + 1]
        m = m_ref[h][:, 0:1]

        Dm = jnp.where(mask, b_col - b_row + ig_row, -jnp.inf)
        inter = b_col + m
        m_t = jnp.maximum(inter, jnp.max(Dm, axis=1, keepdims=True))

        q_pair = q[:, p * LANES:(p + 1) * LANES]
        own = (sub >= half * MLSTM_DK) & (sub < (half + 1) * MLSTM_DK)
        kT_h = jnp.where(own, kT[p * LANES:(p + 1) * LANES, :], 0.0)
        qk = _dot(q_pair, kT_h.astype(BF))
        w = jnp.exp(Dm - m_t) * qk
        v_ext = jnp.concatenate([v[:, h * LANES:(h + 1) * LANES], ones_col], axis=1)
        wv = _dot(w.astype(BF), v_ext)
        st = state_ref[h]
        qs = _dot(q_pair, st.astype(BF))
        s_inter = jnp.exp(inter - m_t)
        tot = s_inter * qs + wv
        num = tot[:, 0:LANES]
        den = tot[:, LANES:LANES + 1]
        hout = num / jnp.maximum(jnp.abs(den), jnp.exp(-m_t))
        o_ref[:, h * LANES:(h + 1) * LANES] = hout.astype(o_ref.dtype)

        a_row = g - b_row + ig_row
        m_new = jnp.maximum(g + m, jnp.max(a_row, axis=1, keepdims=True))
        wk_row = jnp.exp(a_row - m_new)
        decay = jnp.exp(g + m - m_new)
        kw = (kT_h * wk_row).astype(BF)
        state_ref[h] = decay * st + _dot(kw, v_ext)
        m_ref[h] = jnp.broadcast_to(m_new, (1, LANES))


def _mlstm(proj, gates, B, S):
    N = proj.shape[0]
    L = MLSTM_CHUNK
    nc = S // L
    H = MLSTM_HEADS
    t = np.arange(L)
    lower = (t[None, :] <= t[:, None]).astype(np.float32)
    tri = jnp.asarray(np.stack([lower, lower.T]), dtype=F32)

    def row(b, d, c):
        return b * nc + c + d * (nc - 1 - 2 * c)

    return pl.pallas_call(
        _mlstm_kernel,
        grid=(B, 2, nc),
        in_specs=[
            pl.BlockSpec((L, H * MLSTM_DK), lambda b, d, c: (row(b, d, c), 0)),
            pl.BlockSpec((L, H * MLSTM_DK), lambda b, d, c: (row(b, d, c), 1)),
            pl.BlockSpec((L, H * MLSTM_DV), lambda b, d, c: (row(b, d, c), 1)),
            pl.BlockSpec((L, LANES), lambda b, d, c: (row(b, d, c), 0)),
            pl.BlockSpec((2, L, L), lambda b, d, c: (0, 0, 0)),
        ],
        out_specs=pl.BlockSpec((None, L, H * MLSTM_DV), lambda b, d, c: (d, row(b, d, c), 0)),
        out_shape=jax.ShapeDtypeStruct((2, N, H * MLSTM_DV), BF),
        scratch_shapes=[
            pltpu.VMEM((H, 2 * MLSTM_DK, 2 * LANES), F32),
            pltpu.VMEM((H, 1, LANES), F32),
        ],
        compiler_params=_cp(("arbitrary", "arbitrary", "arbitrary")),
        name="mlstm",
    )(proj, proj, proj, gates, tri)


def _outproj_tail(a, x_ref, g1_ref, w_ref, n2_ref, sc2_ref, sh2_ref, wr_ref, xo_ref, h2_ref, lg_ref):
    xn = x_ref[...] + g1_ref[...] * _dot(a, w_ref[...])
    xo_ref[...] = xn
    h2 = _norm_mod(xn, n2_ref[...], sc2_ref[...], sh2_ref[...]).astype(BF)
    h2_ref[...] = h2
    lg = _dot_nt(wr_ref[...], h2)
    for cb in range(lg.shape[1] // LANES):
        lg_ref[cb] = lg[:, cb * LANES:(cb + 1) * LANES]


def _outproj_attn_kernel(a_ref, x_ref, g1_ref, w_ref, n2_ref, sc2_ref, sh2_ref, wr_ref,
                         xo_ref, h2_ref, lg_ref):
    _outproj_tail(a_ref[...], x_ref, g1_ref, w_ref, n2_ref, sc2_ref, sh2_ref, wr_ref,
                  xo_ref, h2_ref, lg_ref)


def _outproj_mlstm_kernel(hf_ref, hb_ref, og_ref, on_ref, bd_ref, x_ref, g1_ref, w_ref, n2_ref,
                          sc2_ref, sh2_ref, wr_ref, xo_ref, h2_ref, lg_ref, a_ref):
    for j in range(D_MODEL // 256):
        sl = slice(j * 256, (j + 1) * 256)
        hs = hf_ref[:, sl].astype(F32) + hb_ref[:, sl].astype(F32)
        ss = _group_sumsq(hs, bd_ref[...])
        hn = hs * lax.rsqrt(ss * (1.0 / MLSTM_DV) + EPS) * on_ref[:, sl]
        a_ref[:, sl] = (hn * _sigmoid(og_ref[:, sl].astype(F32))).astype(BF)
    _outproj_tail(a_ref[...], x_ref, g1_ref, w_ref, n2_ref, sc2_ref, sh2_ref, wr_ref,
                  xo_ref, h2_ref, lg_ref)


def _outproj(kind, mix_inputs, x2, S, g1, w_bf, n2, sc2, sh2, wrT_bf, tm=512):
    N = x2.shape[0]
    spb = S // tm
    row = lambda i: (i, 0)
    const = lambda i: (0, 0)
    per_b = lambda i: (i // spb, 0, 0)
    tail_specs = [
        pl.BlockSpec((tm, D_MODEL), row),
        pl.BlockSpec((None, 1, D_MODEL), per_b),
        pl.BlockSpec((D_MODEL, D_MODEL), const),
        pl.BlockSpec((1, D_MODEL), const),
        pl.BlockSpec((None, 1, D_MODEL), per_b),
        pl.BlockSpec((None, 1, D_MODEL), per_b),
        pl.BlockSpec((N_EXPERTS, D_MODEL), const),
    ]
    out_specs = [
        pl.BlockSpec((tm, D_MODEL), row),
        pl.BlockSpec((tm, D_MODEL), row),
        pl.BlockSpec((tm // LANES, N_EXPERTS, LANES), lambda i: (i, 0, 0)),
    ]
    out_shape = [
        jax.ShapeDtypeStruct((N, D_MODEL), F32),
        jax.ShapeDtypeStruct((N, D_MODEL), BF),
        jax.ShapeDtypeStruct((N // LANES, N_EXPERTS, LANES), F32),
    ]
    tail_args = (x2, g1, w_bf, n2, sc2, sh2, wrT_bf)
    if kind == "attn":
        (a,) = mix_inputs
        return pl.pallas_call(
            _outproj_attn_kernel,
            grid=(N // tm,),
            in_specs=[pl.BlockSpec((tm, D_MODEL), row)] + tail_specs,
            out_specs=out_specs, out_shape=out_shape,
            compiler_params=_cp(("arbitrary",)),
            name="outproj_attn",
        )(a, *tail_args)
    hfb, proj, onorm_row = mix_inputs
    bd = _block_diag_ones(MLSTM_DV)
    return pl.pallas_call(
        _outproj_mlstm_kernel,
        grid=(N // tm,),
        in_specs=[
            pl.BlockSpec((None, tm, D_MODEL), lambda i: (0, i, 0)),
            pl.BlockSpec((None, tm, D_MODEL), lambda i: (1, i, 0)),
            pl.BlockSpec((tm, D_MODEL), lambda i: (i, 2)),
            pl.BlockSpec((1, D_MODEL), const),
            pl.BlockSpec((256, 256), const),
        ] + tail_specs,
        out_specs=out_specs, out_shape=out_shape,
        scratch_shapes=[pltpu.VMEM((tm, D_MODEL), BF)],
        compiler_params=_cp(("arbitrary",)),
        name="outproj_mlstm",
    )(hfb, hfb, proj, onorm_row, bd, *tail_args)


def _router_kernel(lg_ref, triu_ref, rank_ref, gate_ref, offs_ref, bits_ref, *, cap, blocks_per_tb):
    NB = lg_ref.shape[0]
    E = N_EXPERTS
    shape = (E, LANES)

    def softmax_body(b, carry):
        l = lg_ref[b]
        e = jnp.exp(l - jnp.max(l, axis=0, keepdims=True))
        aff = e / jnp.sum(e, axis=0, keepdims=True)
        gate_ref[b] = aff
        bits_ref[b] = pltpu.bitcast(aff, jnp.int32)
        return carry

    lax.fori_loop(0, NB, softmax_body, 0)

    def count(pred):
        def body(b, acc):
            return acc + jnp.where(pred(bits_ref[b]), 1.0, 0.0)
        acc = lax.fori_loop(0, NB, body, jnp.zeros(shape, F32))
        return jnp.broadcast_to(jnp.sum(acc, axis=1, keepdims=True), shape)

    def search_body(i, T):
        cand = T | jnp.left_shift(jnp.int32(1), 30 - i)
        cnt = count(lambda x: x >= cand)
        return jnp.where(cnt >= cap, cand, T)

    T = lax.fori_loop(0, 31, search_body, jnp.zeros(shape, jnp.int32))
    need = cap - count(lambda x: x > T)

    triu = triu_ref[...]
    ones = jnp.ones((LANES, LANES), BF)

    def tb_body(tb, carry):
        ceq, csel = carry
        offs_ref[tb] = csel
        for k in range(blocks_per_tb):
            b = tb * blocks_per_tb + k
            x = bits_ref[b]
            gt = x > T
            eq = x == T
            eqf = jnp.where(eq, 1.0, 0.0)
            eqb = eqf.astype(BF)
            rank_eq = ceq + _dot(eqb, triu) - eqf
            sel = gt | (eq & (rank_eq < need))
            self_ = jnp.where(sel, 1.0, 0.0)
            selb = self_.astype(BF)
            rank = csel + _dot(selb, triu) - self_
            rank_ref[b] = jnp.where(sel, rank, -1.0).astype(jnp.int32)
            gate_ref[b] = jnp.where(sel, gate_ref[b], 0.0)
            ceq = ceq + _dot(eqb, ones)
            csel = csel + _dot(selb, ones)
        return ceq, csel

    n_tb = NB // blocks_per_tb
    _, csel = lax.fori_loop(0, n_tb, tb_body, (jnp.zeros(shape, F32), jnp.zeros(shape, F32)))
    offs_ref[n_tb] = csel


def _router(lg, cap, tb_tokens):
    NB = lg.shape[0]
    bpt = tb_tokens // LANES
    n_tb = NB // bpt
    u = np.arange(LANES)
    triu = jnp.asarray((u[:, None] <= u[None, :]).astype(np.float32), dtype=BF)
    kern = functools.partial(_router_kernel, cap=float(cap), blocks_per_tb=bpt)
    return pl.pallas_call(
        kern,
        out_shape=[
            jax.ShapeDtypeStruct((NB, N_EXPERTS, LANES), jnp.int32),
            jax.ShapeDtypeStruct((NB, N_EXPERTS, LANES), F32),
            jax.ShapeDtypeStruct((n_tb + 1, N_EXPERTS, LANES), F32),
        ],
        scratch_shapes=[pltpu.VMEM((NB, N_EXPERTS, LANES), jnp.int32)],
        compiler_params=pltpu.CompilerParams(vmem_limit_bytes=VMEM_LIMIT),
        name="router",
    )(lg, triu)


def _schedules(offs, cap, T, n_t):
    E = N_EXPERTS
    n_s = cap // T
    lo = offs[:-1]
    hi = offs[1:]
    nonempty = hi > lo
    s_lo = jnp.minimum(lo // T, n_s - 1)
    s_hi = jnp.where(nonempty, (hi - 1) // T, s_lo)
    cnt = jnp.where(nonempty, s_hi - s_lo + 1, 0)

    def build(cnt_flat, e_flat, t_flat, slo_flat, n_steps):
        ends = jnp.cumsum(cnt_flat)
        total = ends[-1]
        i = jnp.arange(n_steps, dtype=jnp.int32)
        ic = jnp.minimum(i, total - 1)
        pair = jnp.searchsorted(ends, ic, side="right").astype(jnp.int32)
        start = ends[pair] - cnt_flat[pair]
        s = slo_flat[pair] + (ic - start)
        return e_flat[pair], s.astype(jnp.int32), t_flat[pair], (i < total)

    e_idx = jnp.broadcast_to(jnp.arange(E, dtype=jnp.int32)[:, None], (E, n_t)).reshape(-1)
    t_idx = jnp.broadcast_to(jnp.arange(n_t, dtype=jnp.int32)[None, :], (E, n_t)).reshape(-1)
    n_f = E * (n_s + n_t)
    fe, fs, ft, fvalid = build(cnt.T.reshape(-1), e_idx, t_idx, s_lo.T.reshape(-1), n_f)
    key = fe * n_s + fs
    ffirst = fvalid & jnp.concatenate([jnp.ones((1,), bool), key[1:] != key[:-1]])
    flast = fvalid & jnp.concatenate([(key[1:] != key[:-1]) | ~fvalid[1:], jnp.ones((1,), bool)])
    fflags = (ffirst.astype(jnp.int32) + 2 * flast.astype(jnp.int32) + 4 * fvalid.astype(jnp.int32))

    cnt_c = cnt.at[:, 0].set(jnp.maximum(cnt[:, 0], 1))
    e_idx2 = jnp.broadcast_to(jnp.arange(E, dtype=jnp.int32)[None, :], (n_t, E)).reshape(-1)
    t_idx2 = jnp.broadcast_to(jnp.arange(n_t, dtype=jnp.int32)[:, None], (n_t, E)).reshape(-1)
    n_c = E * (n_s + n_t) + n_t
    ce, cs, ct, cvalid = build(cnt_c.reshape(-1), e_idx2, t_idx2, s_lo.reshape(-1), n_c)
    cfirst = cvalid & jnp.concatenate([jnp.ones((1,), bool), ct[1:] != ct[:-1]])
    clast = cvalid & jnp.concatenate([(ct[1:] != ct[:-1]) | ~cvalid[1:], jnp.ones((1,), bool)])
    cflags = (cfirst.astype(jnp.int32) + 2 * clast.astype(jnp.int32) + 4 * cvalid.astype(jnp.int32))
    return (fe, fs, ft, fflags), (ce, cs, ct, cflags)


def _one_hot_slots(rank_ref, e, s, T, n_blocks):
    slot = s * T + lax.broadcasted_iota(jnp.int32, (T, LANES), 0)
    pieces = [jnp.where(rank_ref[a, pl.ds(e, 1), :] == slot, 1.0, 0.0).astype(BF) for a in range(n_blocks)]
    return jnp.concatenate(pieces, axis=1)


def _ffn_kernel(se_ref, ss_ref, st_ref, fl_ref, x_ref, rank_ref, gate_ref, wg_ref, wu_ref, wd_ref,
                ye_ref, xacc_ref, gacc_ref, *, T, n_blocks, f_chunk):
    i = pl.program_id(0)
    e = se_ref[i]
    s = ss_ref[i]
    fl = fl_ref[i]

    @pl.when((fl & 1) != 0)
    def _():
        xacc_ref[...] = jnp.zeros(xacc_ref.shape, F32)
        gacc_ref[...] = jnp.zeros(gacc_ref.shape, F32)

    @pl.when((fl & 4) != 0)
    def _():
        P = _one_hot_slots(rank_ref, e, s, T, n_blocks)
        xacc_ref[...] += _dot(P, x_ref[...])
        g = jnp.concatenate([gate_ref[a, pl.ds(e, 1), :] for a in range(n_blocks)], axis=1)
        parts = [t.astype(F32) for t in _split3(g)]
        grows = jnp.concatenate(parts + [jnp.zeros((13, g.shape[1]), F32)], axis=0).astype(BF)
        gacc_ref[...] += _dot_nt(P, grows)

    @pl.when((fl & 2) != 0)
    def _():
        x = xacc_ref[...].astype(BF)
        y = jnp.zeros((T, D_MODEL), F32)
        for c in range(EXPERT_FF // f_chunk):
            sl = slice(c * f_chunk, (c + 1) * f_chunk)
            gt = _dot(x, wg_ref[:, sl])
            up = _dot(x, wu_ref[:, sl])
            hid = (gt * _sigmoid(gt) * up).astype(BF)
            y = y + _dot(hid, wd_ref[sl, :])
        ga = gacc_ref[...]
        gcol = ga[:, 0:1] + ga[:, 1:2] + ga[:, 2:3]
        ye_ref[...] = (y * gcol).astype(ye_ref.dtype)


def _ffn(sched, h2, rankm, gate, wg, wu, wd, cap, T, TB):
    se, ss, st, fl = sched
    n_steps = se.shape[0]
    nb = TB // LANES
    kern = functools.partial(_ffn_kernel, T=T, n_blocks=nb, f_chunk=512)
    grid_spec = pltpu.PrefetchScalarGridSpec(
        num_scalar_prefetch=4,
        grid=(n_steps,),
        in_specs=[
            pl.BlockSpec((TB, D_MODEL), lambda i, se, ss, st, fl: (st[i], 0)),
            pl.BlockSpec((nb, N_EXPERTS, LANES), lambda i, se, ss, st, fl: (st[i], 0, 0)),
            pl.BlockSpec((nb, N_EXPERTS, LANES), lambda i, se, ss, st, fl: (st[i], 0, 0)),
            pl.BlockSpec((None, D_MODEL, EXPERT_FF), lambda i, se, ss, st, fl: (se[i], 0, 0)),
            pl.BlockSpec((None, D_MODEL, EXPERT_FF), lambda i, se, ss, st, fl: (se[i], 0, 0)),
            pl.BlockSpec((None, EXPERT_FF, D_MODEL), lambda i, se, ss, st, fl: (se[i], 0, 0)),
        ],
        out_specs=pl.BlockSpec((None, T, D_MODEL), lambda i, se, ss, st, fl: (se[i], ss[i], 0)),
        scratch_shapes=[pltpu.VMEM((T, D_MODEL), F32), pltpu.VMEM((T, 16), F32)],
    )
    return pl.pallas_call(
        kern,
        grid_spec=grid_spec,
        out_shape=jax.ShapeDtypeStruct((N_EXPERTS, cap, D_MODEL), BF),
        compiler_params=_cp(("arbitrary",)),
        name="moe_ffn",
    )(se, ss, st, fl, h2, rankm, gate, wg, wu, wd)


def _combine_kernel(se_ref, ss_ref, st_ref, fl_ref, ye_ref, rank_ref, x_ref, g2_ref, o_ref, acc_ref,
                    *, T, n_blocks):
    i = pl.program_id(0)
    e = se_ref[i]
    s = ss_ref[i]
    fl = fl_ref[i]

    @pl.when((fl & 1) != 0)
    def _():
        acc_ref[...] = jnp.zeros(acc_ref.shape, F32)

    @pl.when((fl & 4) != 0)
    def _():
        P = _one_hot_slots(rank_ref, e, s, T, n_blocks)
        acc_ref[...] += _dot_tn(P, ye_ref[...])

    @pl.when((fl & 2) != 0)
    def _():
        o_ref[...] = x_ref[...] + g2_ref[...] * acc_ref[...]


def _combine(sched, ye, rankm, x1, g2, S, T, TB):
    se, ss, st, fl = sched
    n_steps = se.shape[0]
    N = x1.shape[0]
    nb = TB // LANES
    spb = S // TB
    kern = functools.partial(_combine_kernel, T=T, n_blocks=nb)
    grid_spec = pltpu.PrefetchScalarGridSpec(
        num_scalar_prefetch=4,
        grid=(n_steps,),
        in_specs=[
            pl.BlockSpec((None, T, D_MODEL), lambda i, se, ss, st, fl: (se[i], ss[i], 0)),
            pl.BlockSpec((nb, N_EXPERTS, LANES), lambda i, se, ss, st, fl: (st[i], 0, 0)),
            pl.BlockSpec((TB, D_MODEL), lambda i, se, ss, st, fl: (st[i], 0)),
            pl.BlockSpec((None, 1, D_MODEL), lambda i, se, ss, st, fl: (st[i] // spb, 0, 0)),
        ],
        out_specs=pl.BlockSpec((TB, D_MODEL), lambda i, se, ss, st, fl: (st[i], 0)),
        scratch_shapes=[pltpu.VMEM((TB, D_MODEL), F32)],
    )
    return pl.pallas_call(
        kern,
        grid_spec=grid_spec,
        out_shape=jax.ShapeDtypeStruct((N, D_MODEL), F32),
        compiler_params=_cp(("arbitrary",)),
        name="moe_combine",
    )(se, ss, st, fl, ye, rankm, x1, g2)


def _moe(h2, lg, x1, g2, S, wg, wu, wd, T, TB):
    N = x1.shape[0]
    cap = (CAPACITY_FACTOR * N) // N_EXPERTS
    T = min(T, cap)
    n_t = N // TB
    rankm, gate, offs = _router(lg, cap, TB)
    offs_i = offs[:, :, 0].astype(jnp.int32)
    fsched, csched = _schedules(offs_i, cap, T, n_t)
    ye = _ffn(fsched, h2, rankm, gate, wg, wu, wd, cap, T, TB)
    return _combine(csched, ye, rankm, x1, g2, S, T, TB)


def _trunk(x, mods, P, T=256, TB=512):
    B, S, _ = x.shape
    x2 = x.reshape(B * S, D_MODEL)
    for l in range(DEPTH):
        mod = mods[l]
        sh1, sc1, g1, sh2, sc2, g2 = [mod[:, j][:, None, :] for j in range(N_MOD)]
        n1 = P["norm_g"][l, 0][None, :]
        n2 = P["norm_g"][l, 1][None, :]
        j = l // 2
        if l % 2 == 0:
            lambda_init = 0.8 - 0.6 * math.exp(-0.3 * l)
            qkv = _inproj_attn(x2, S, n1, sc1, sh1, P["attn_w_in"][j], P["attn_gain"][j])
            a = _attention(qkv, B, S, P["slopes"], P["attn_dmax"][j], P["attn_lambda"][j],
                           P["attn_subln"][j][None, :], lambda_init)
            x1, h2, lg = _outproj("attn", (a,), x2, S, g1, P["attn_w_out"][j], n2, sc2, sh2,
                                  P["w_routerT"][l])
        else:
            proj, gates = _inproj_mlstm(x2, S, n1, sc1, sh1, P["mlstm_w_main"][j], P["mlstm_w_gate"][j],
                                        P["mlstm_b_gate"][j])
            hfb = _mlstm(proj, gates, B, S)
            x1, h2, lg = _outproj("mlstm", (hfb, proj, P["mlstm_out_norm"][j]), x2, S, g1,
                                  P["mlstm_w_out"][j], n2, sc2, sh2, P["w_routerT"][l])
        x2 = _moe(h2, lg, x1, g2, S, P["w_exp_gate"][l], P["w_exp_up"][l], P["w_exp_down"][l], T, TB)
    return x2.reshape(B, S, D_MODEL)


def _prepare(norm_g, attn_w_in, attn_q_gain, attn_k_gain, attn_lambda, attn_subln, attn_w_out,
             mlstm_w_in, mlstm_b_gate, mlstm_out_norm, mlstm_w_out, w_router, w_exp_gate, w_exp_up,
             w_exp_down):
    H = ATTN_HEADS
    n_main = 2 * MLSTM_HEADS * MLSTM_DK + MLSTM_HEADS * MLSTM_DV + D_MODEL
    n_gate = 4 * MLSTM_HEADS
    qg = jnp.tile(attn_q_gain * (ATTN_HEAD_DIM ** -0.5 * _LOG2E), (1, 2 * H))
    kg = jnp.tile(attn_k_gain, (1, 2 * H))
    slopes = 2.0 ** (-8.0 * np.arange(1, H + 1) / H)
    assert all(np.frexp(slopes)[0] == 0.5), "ALiBi slopes must be powers of two for bf16-exact features"
    smax = (8.0 * _LOG2E * 1.02) * jnp.max(jnp.abs(attn_q_gain), axis=1) * jnp.max(jnp.abs(attn_k_gain), axis=1)
    dmax = jnp.ceil((152.0 + 2.0 * smax[:, None]) / jnp.asarray(slopes * _LOG2E, F32)[None, :])
    dmax = jnp.clip(dmax, 0.0, 2.0 ** 30).astype(jnp.int32)
    return {
        "attn_dmax": dmax,
        "norm_g": norm_g,
        "attn_w_in": attn_w_in.astype(BF),
        "attn_gain": jnp.concatenate([qg, kg], axis=1)[:, None, :],
        "attn_lambda": attn_lambda,
        "attn_subln": attn_subln,
        "attn_w_out": attn_w_out.astype(BF),
        "slopes": jnp.asarray(slopes, F32),
        "mlstm_w_main": mlstm_w_in[:, :, :n_main].astype(BF),
        "mlstm_w_gate": jnp.pad(mlstm_w_in[:, :, n_main:], ((0, 0), (0, 0), (0, LANES - n_gate))).astype(BF),
        "mlstm_b_gate": jnp.pad(mlstm_b_gate, ((0, 0), (0, LANES - n_gate)))[:, None, :],
        "mlstm_out_norm": jnp.tile(mlstm_out_norm, (1, MLSTM_HEADS))[:, None, :],
        "mlstm_w_out": mlstm_w_out.astype(BF),
        "w_routerT": jnp.swapaxes(w_router, 1, 2).astype(BF),
        "w_exp_gate": w_exp_gate.astype(BF),
        "w_exp_up": w_exp_up.astype(BF),
        "w_exp_down": w_exp_down.astype(BF),
    }


def kernel(x_prompt, x_sample, c_prompt, c_sample, norm_g, w_ada, b_ada, attn_w_in, attn_q_gain,
           attn_k_gain, attn_lambda, attn_subln, attn_w_out, mlstm_w_in, mlstm_b_gate, mlstm_out_norm,
           mlstm_w_out, w_router, w_exp_gate, w_exp_up, w_exp_down):
    P = _prepare(norm_g, attn_w_in, attn_q_gain, attn_k_gain, attn_lambda, attn_subln, attn_w_out,
                 mlstm_w_in, mlstm_b_gate, mlstm_out_norm, mlstm_w_out, w_router, w_exp_gate,
                 w_exp_up, w_exp_down)
    bp, bs = c_prompt.shape[0], c_sample.shape[0]
    pad = (-(bp + bs)) % 8
    c_all = jnp.concatenate([c_prompt, c_sample, jnp.zeros((pad, D_MODEL), F32)], axis=0)
    mods = _ada(c_all, w_ada, b_ada)
    mods = mods.reshape(DEPTH, c_all.shape[0], N_MOD, D_MODEL)
    y_prompt = _trunk(x_prompt, mods[:, :bp], P)
    y_sample = _trunk(x_sample, mods[:, bp:bp + bs], P)
    return (y_prompt, y_sample)
```

```python
import functools
import math

import jax
import jax.numpy as jnp
import numpy as np
from jax import lax
from jax.experimental import pallas as pl
from jax.experimental.pallas import tpu as pltpu

D_MODEL = 1024
DEPTH = 4
ATTN_HEADS = 8
ATTN_HEAD_DIM = 64
MLSTM_HEADS = 8
MLSTM_DK = 64
MLSTM_DV = 128
MLSTM_CHUNK = 128
N_EXPERTS = 16
CAPACITY_FACTOR = 2
EXPERT_FF = 2048
N_MOD = 6
EPS = 1e-6

LANES = 128
VMEM_LIMIT = 56 * 1024 * 1024

BF = jnp.bfloat16
F32 = jnp.float32
NEG = -1e30


def _cp(sem, vmem=VMEM_LIMIT):
    return pltpu.CompilerParams(dimension_semantics=sem, vmem_limit_bytes=vmem)


def _sigmoid(x):
    return 1.0 / (1.0 + jnp.exp(-x))


def _dot(a, b):
    return jnp.dot(a, b, preferred_element_type=F32)


def _dot_nt(a, b):
    return lax.dot_general(a, b, (((1,), (1,)), ((), ())), preferred_element_type=F32)


def _dot_tn(a, b):
    return lax.dot_general(a, b, (((0,), (0,)), ((), ())), preferred_element_type=F32)


def _split3(x):
    a = x.astype(BF)
    r = x - a.astype(F32)
    b = r.astype(BF)
    c = (r - b.astype(F32)).astype(BF)
    return a, b, c


def _ada_kernel(c_ref, w_ref, b_ref, o_ref):
    c = c_ref[...]
    s = (c * _sigmoid(c)).astype(BF)
    o_ref[...] = _dot(s, w_ref[...].astype(BF)) + b_ref[...]


def _ada(c_all, w_ada, b_ada):
    bp = c_all.shape[0]
    n_out = N_MOD * D_MODEL
    tn = 1536
    return pl.pallas_call(
        _ada_kernel,
        grid=(DEPTH, n_out // tn),
        in_specs=[
            pl.BlockSpec((bp, D_MODEL), lambda l, j: (0, 0)),
            pl.BlockSpec((None, D_MODEL, tn), lambda l, j: (l, 0, j)),
            pl.BlockSpec((None, 1, tn), lambda l, j: (l, 0, j)),
        ],
        out_specs=pl.BlockSpec((None, bp, tn), lambda l, j: (l, 0, j)),
        out_shape=jax.ShapeDtypeStruct((DEPTH, bp, n_out), F32),
        compiler_params=_cp(("arbitrary", "arbitrary")),
        name="ada",
    )(c_all, w_ada, b_ada.reshape(DEPTH, 1, n_out))


def _norm_mod(x, g, sc, sh):
    r = lax.rsqrt(jnp.mean(x * x, axis=-1, keepdims=True) + EPS)
    return x * r * g * (1.0 + sc) + sh


def _group_sumsq(y, bd):
    return _dot((y * y).astype(BF), bd)


def _block_diag_ones(group, size=256):
    i = np.arange(size) // group
    return jnp.asarray((i[:, None] == i[None, :]).astype(np.float32), dtype=BF)


def _inproj_attn_kernel(x_ref, g_ref, sc_ref, sh_ref, w_ref, gain_ref, bd_ref, o_ref):
    h = _norm_mod(x_ref[...], g_ref[...], sc_ref[...], sh_ref[...]).astype(BF)
    n_chunks = o_ref.shape[1] // 256
    n_norm = gain_ref.shape[1] // 256
    for j in range(n_chunks):
        sl = slice(j * 256, (j + 1) * 256)
        y = _dot(h, w_ref[:, sl])
        if j < n_norm:
            ss = _group_sumsq(y, bd_ref[...])
            y = y * lax.rsqrt(ss * (1.0 / ATTN_HEAD_DIM) + EPS) * gain_ref[:, sl]
        o_ref[:, sl] = y.astype(o_ref.dtype)


def _inproj_attn(x2, S, g, sc, sh, w_bf, gain_row, tm=512):
    N = x2.shape[0]
    n_out = w_bf.shape[1]
    spb = S // tm
    bd = _block_diag_ones(ATTN_HEAD_DIM)
    return pl.pallas_call(
        _inproj_attn_kernel,
        grid=(N // tm,),
        in_specs=[
            pl.BlockSpec((tm, D_MODEL), lambda i: (i, 0)),
            pl.BlockSpec((1, D_MODEL), lambda i: (0, 0)),
            pl.BlockSpec((None, 1, D_MODEL), lambda i: (i // spb, 0, 0)),
            pl.BlockSpec((None, 1, D_MODEL), lambda i: (i // spb, 0, 0)),
            pl.BlockSpec((D_MODEL, n_out), lambda i: (0, 0)),
            pl.BlockSpec((1, gain_row.shape[1]), lambda i: (0, 0)),
            pl.BlockSpec((256, 256), lambda i: (0, 0)),
        ],
        out_specs=pl.BlockSpec((tm, n_out), lambda i: (i, 0)),
        out_shape=jax.ShapeDtypeStruct((N, n_out), BF),
        compiler_params=_cp(("arbitrary",)),
        name="inproj_attn",
    )(x2, g, sc, sh, w_bf, gain_row, bd)


def _inproj_mlstm_kernel(x_ref, g_ref, sc_ref, sh_ref, w_ref, wg_ref, bg_ref, o_ref, og_ref):
    h = _norm_mod(x_ref[...], g_ref[...], sc_ref[...], sh_ref[...]).astype(BF)
    hk = MLSTM_HEADS * MLSTM_DK
    for j in range(o_ref.shape[1] // 256):
        sl = slice(j * 256, (j + 1) * 256)
        y = _dot(h, w_ref[:, sl])
        if hk <= j * 256 < 2 * hk:
            y = y * (MLSTM_DK ** -0.5)
        o_ref[:, sl] = y.astype(o_ref.dtype)
    og_ref[...] = _dot(h, wg_ref[...]) + bg_ref[...]


def _inproj_mlstm(x2, S, g, sc, sh, w_bf, wg_bf, bg_row, tm=512):
    N = x2.shape[0]
    n_out = w_bf.shape[1]
    spb = S // tm
    return pl.pallas_call(
        _inproj_mlstm_kernel,
        grid=(N // tm,),
        in_specs=[
            pl.BlockSpec((tm, D_MODEL), lambda i: (i, 0)),
            pl.BlockSpec((1, D_MODEL), lambda i: (0, 0)),
            pl.BlockSpec((None, 1, D_MODEL), lambda i: (i // spb, 0, 0)),
            pl.BlockSpec((None, 1, D_MODEL), lambda i: (i // spb, 0, 0)),
            pl.BlockSpec((D_MODEL, n_out), lambda i: (0, 0)),
            pl.BlockSpec((D_MODEL, LANES), lambda i: (0, 0)),
            pl.BlockSpec((1, LANES), lambda i: (0, 0)),
        ],
        out_specs=[
            pl.BlockSpec((tm, n_out), lambda i: (i, 0)),
            pl.BlockSpec((tm, LANES), lambda i: (i, 0)),
        ],
        out_shape=[
            jax.ShapeDtypeStruct((N, n_out), BF),
            jax.ShapeDtypeStruct((N, LANES), F32),
        ],
        compiler_params=_cp(("arbitrary",)),
        name="inproj_mlstm",
    )(x2, g, sc, sh, w_bf, wg_bf, bg_row)


_LOG2E = math.log2(math.e)
N_FEAT = 12


def _log2e_split():
    c = np.float32(_LOG2E)
    out = []
    for _ in range(3):
        t = np.asarray(c, dtype=BF).astype(np.float32)
        out.append(float(t))
        c = np.float32(c - t)
    return out


def _attn_tables(S):
    c = _log2e_split()
    pos = np.arange(S)
    kf = np.zeros((S, LANES), np.float32)
    qc = np.zeros((1, LANES), np.float32)
    for t in range(3):
        kf[:, t] = -128.0 * c[t]
        kf[:, 3 + t] = -c[t]
        kf[:, 6 + t] = pos // 128
        kf[:, 9 + t] = pos % 128
        qc[0, 6 + t] = 128.0 * c[t]
        qc[0, 9 + t] = c[t]
    return jnp.asarray(kf, dtype=BF), jnp.asarray(qc, dtype=F32)


def _attn_kernel(slopes_ref, dmax_ref, q_ref, k_ref, v_ref, kf_ref, qc_ref, lam_ref, subln_ref, o_ref,
                 qs_ref, m_ref, l_ref, acc_ref, *, tq, tk, S, lambda_init):
    h = pl.program_id(1)
    qi = pl.program_id(2)
    nk = S // tk
    slope = slopes_ref[h]
    r0 = qi * tq

    q = q_ref[...]
    lane = lax.broadcasted_iota(jnp.int32, (tq, LANES), 1)
    zero = jnp.zeros_like(q)
    q0 = jnp.where(lane < ATTN_HEAD_DIM, q, zero)
    q1 = jnp.where(lane >= ATTN_HEAD_DIM, q, zero)
    pos = r0 + lax.broadcasted_iota(jnp.int32, (tq, LANES), 0)
    hi = (pos >> 7).astype(F32)
    lo = (pos & 127).astype(F32)
    feat = jnp.where(lane < 3, hi, jnp.where(lane < 6, lo, qc_ref[...])) * slope
    for sgn, f in ((0, feat.astype(BF)), (1, (-feat).astype(BF))):
        qs_ref[sgn, 0:tq, 0:LANES] = q0
        qs_ref[sgn, tq:2 * tq, 0:LANES] = q1
        qs_ref[sgn, 0:tq, LANES:2 * LANES] = f
        qs_ref[sgn, tq:2 * tq, LANES:2 * LANES] = f

    def scores(ki, sgn):
        off = pl.multiple_of(ki * tk, tk)
        kaug = jnp.concatenate([k_ref[pl.ds(off, tk), :], kf_ref[pl.ds(off, tk), :]], axis=1)
        return _dot_nt(qs_ref[sgn], kaug), off

    def chunk(ki, sgn, state):
        m_prev, l_prev, acc_prev = state
        s, off = scores(ki, sgn)
        m_new = jnp.maximum(m_prev, jnp.max(s, axis=1, keepdims=True))
        p = jnp.exp2(s - jnp.concatenate([m_new] * (tk // LANES), axis=1))
        alpha = jnp.exp2(m_prev - m_new)
        l_new = alpha * l_prev + jnp.sum(p, axis=1, keepdims=True)
        acc_new = alpha * acc_prev + _dot(p.astype(BF), v_ref[pl.ds(off, tk), :])
        return m_new, l_new, acc_new

    kd = r0 // tk
    s, off = scores(kd, 0)
    row = lax.broadcasted_iota(jnp.int32, (tq, tk), 0)
    col = lax.broadcasted_iota(jnp.int32, (tq, tk), 1)
    corr = jnp.minimum(row - col + (r0 - off), 0).astype(F32) * (2.0 * _LOG2E * slope)
    s = s + jnp.concatenate([corr, corr], axis=0)
    m0 = jnp.max(s, axis=1, keepdims=True)
    p = jnp.exp2(s - m0)
    m_ref[...] = jnp.broadcast_to(m0, m_ref.shape)
    l_ref[...] = jnp.broadcast_to(jnp.sum(p, axis=1, keepdims=True), l_ref.shape)
    acc_ref[...] = _dot(p.astype(BF), v_ref[pl.ds(off, tk), :])

    dmax = dmax_ref[h]
    k_lo = jnp.maximum(r0 - dmax, 0) // tk
    k_hi = jnp.minimum((r0 + tq - 1 + dmax) // tk, nk - 1)
    n_before = kd - k_lo
    n_off = n_before + (k_hi - kd)

    def nth(i):
        after = (i >= n_before).astype(jnp.int32)
        return k_lo + i + after, after

    def load_state():
        return m_ref[...], l_ref[...], acc_ref[...]

    def store_state(state):
        m_ref[...], l_ref[...], acc_ref[...] = state

    def pair_body(j, c):
        state = load_state()
        state = chunk(*nth(2 * j), state)
        state = chunk(*nth(2 * j + 1), state)
        store_state(state)
        return c

    lax.fori_loop(0, n_off // 2, pair_body, 0)

    @pl.when(n_off % 2 == 1)
    def _():
        store_state(chunk(*nth(n_off - 1), load_state()))


    lam = lam_ref[...]
    l01 = jnp.sum(lam[0:1, :] * lam[1:2, :], axis=1, keepdims=True)
    l23 = jnp.sum(lam[2:3, :] * lam[3:4, :], axis=1, keepdims=True)
    lmbda = jnp.exp(l01) - jnp.exp(l23) + lambda_init
    o0 = acc_ref[0:tq, :] / l_ref[0:tq, :]
    o1 = acc_ref[tq:2 * tq, :] / l_ref[tq:2 * tq, :]
    o = o0 - lmbda * o1
    r = lax.rsqrt(jnp.mean(o * o, axis=-1, keepdims=True) + EPS)
    o_ref[...] = (o * r * subln_ref[...] * (1.0 - lambda_init)).astype(o_ref.dtype)


def _attention(qkv, B, S, slopes, dmax, lam, subln_row, lambda_init, tq=512, tk=512):
    N = qkv.shape[0]
    H = ATTN_HEADS
    tq, tk = min(tq, S), min(tk, S)
    assert tk % tq == 0 and S % tk == 0 and S <= 128 * 256
    nq = S // tq
    kf, qc = _attn_tables(S)
    kern = functools.partial(_attn_kernel, tq=tq, tk=tk, S=S, lambda_init=lambda_init)
    grid_spec = pltpu.PrefetchScalarGridSpec(
        num_scalar_prefetch=2,
        grid=(B, H, nq),
        in_specs=[
            pl.BlockSpec((tq, LANES), lambda b, h, qi, sl, dm: (b * nq + qi, h)),
            pl.BlockSpec((S, LANES), lambda b, h, qi, sl, dm: (b, H + h)),
            pl.BlockSpec((S, LANES), lambda b, h, qi, sl, dm: (b, 2 * H + h)),
            pl.BlockSpec((S, LANES), lambda b, h, qi, sl, dm: (0, 0)),
            pl.BlockSpec((1, LANES), lambda b, h, qi, sl, dm: (0, 0)),
            pl.BlockSpec(lam.shape, lambda b, h, qi, sl, dm: (0, 0)),
            pl.BlockSpec((1, LANES), lambda b, h, qi, sl, dm: (0, 0)),
        ],
        out_specs=pl.BlockSpec((tq, LANES), lambda b, h, qi, sl, dm: (b * nq + qi, h)),
        scratch_shapes=[
            pltpu.VMEM((2, 2 * tq, 2 * LANES), BF),
            pltpu.VMEM((2 * tq, LANES), F32),
            pltpu.VMEM((2 * tq, LANES), F32),
            pltpu.VMEM((2 * tq, LANES), F32),
        ],
    )
    return pl.pallas_call(
        kern,
        grid_spec=grid_spec,
        out_shape=jax.ShapeDtypeStruct((N, D_MODEL), BF),
        compiler_params=_cp(("arbitrary", "arbitrary", "arbitrary")),
        name="diff_attn",
    )(slopes, dmax, qkv, qkv, qkv, kf, qc, lam, subln_row)


def _log_sigmoid(x):
    return jnp.minimum(x, 0.0) - jnp.log(1.0 + jnp.exp(-jnp.abs(x)))


def _mlstm_kernel(q_ref, k_ref, v_ref, gate_ref, tri_ref, sel_ref, o_ref, state_ref, m_ref):
    d = pl.program_id(1)
    c = pl.program_id(2)
    L = MLSTM_CHUNK
    H = MLSTM_HEADS

    @pl.when(c == 0)
    def _():
        state_ref[...] = jnp.zeros(state_ref.shape, F32)
        m_ref[...] = jnp.zeros(m_ref.shape, F32)

    G = gate_ref[...]
    G = jnp.where(d == 0, G, pltpu.roll(G, LANES - 2 * H, axis=1))
    lane = lax.broadcasted_iota(jnp.int32, G.shape, 1)
    A = jnp.where((lane >= H) & (lane < 2 * H), _log_sigmoid(G), G)
    AT = A.T
    tri_f = tri_ref[d]
    tri_d = tri_f.astype(BF)
    tri_o = tri_ref[1 - d].astype(BF)
    a1, a2, a3 = _split3(A)
    Bcol = _dot(tri_d, a1) + _dot(tri_d, a2) + _dot(tri_d, a3)
    t1, t2, t3 = _split3(AT)
    Brow = _dot(t1, tri_o) + _dot(t2, tri_o) + _dot(t3, tri_o)
    b1, b2, b3 = _split3(Bcol)
    mask = tri_f > 0.0

    kT = k_ref[...].astype(F32).T
    q = q_ref[...]
    v = v_ref[...]
    sub = lax.broadcasted_iota(jnp.int32, (2 * MLSTM_DK, L), 0)
    ones = jnp.ones((L, LANES), BF)

    heads = range(H)
    ig_row = [AT[h:h + 1, :] for h in heads]
    b_row = [Brow[H + h:H + h + 1, :] for h in heads]
    q_pair = [q[:, (h // 2) * LANES:(h // 2 + 1) * LANES] for h in heads]
    v_ext = [jnp.concatenate([v[:, h * LANES:(h + 1) * LANES], ones], axis=1) for h in heads]
    m = [m_ref[h] for h in heads]
    st = [state_ref[h] for h in heads]
    kT_h = []
    for h in heads:
        own = (sub >= (h % 2) * MLSTM_DK) & (sub < (h % 2 + 1) * MLSTM_DK)
        kT_h.append(jnp.where(own, kT[(h // 2) * LANES:(h // 2 + 1) * LANES, :], 0.0))
    b_rep = [_dot(b1, sel_ref[h]) + _dot(b2, sel_ref[h]) + _dot(b3, sel_ref[h]) for h in heads]
    qk = [_dot(q_pair[h], kT_h[h].astype(BF)) for h in heads]
    qs = [_dot(q_pair[h], st[h].astype(BF)) for h in heads]
    g = [jnp.where(d == 0, b_rep[h][L - 1:L, :], b_rep[h][0:1, :]) for h in heads]
    Dm = [jnp.where(mask, b_rep[h] - (b_row[h] - ig_row[h]), -jnp.inf) for h in heads]
    inter = [b_rep[h] + m[h] for h in heads]
    m_t = [jnp.maximum(inter[h], jnp.max(Dm[h], axis=1, keepdims=True)) for h in heads]
    w = [jnp.exp(Dm[h] - m_t[h]) * qk[h] for h in heads]
    wv = [_dot(w[h].astype(BF), v_ext[h]) for h in heads]
    for h in heads:
        s_inter = jnp.exp(inter[h] - m_t[h])
        num = s_inter * qs[h][:, 0:LANES] + wv[h][:, 0:LANES]
        den = s_inter * qs[h][:, LANES:2 * LANES] + wv[h][:, LANES:2 * LANES]
        hout = num / jnp.maximum(jnp.abs(den), jnp.exp(-m_t[h]))
        o_ref[:, h * LANES:(h + 1) * LANES] = hout.astype(o_ref.dtype)
    for h in heads:
        a_row = g[h] - b_row[h] + ig_row[h]
        m_new = jnp.maximum(g[h] + m[h], jnp.max(a_row, axis=1, keepdims=True))
        wk_row = jnp.exp(a_row - m_new)
        decay = jnp.exp(g[h] + m[h] - m_new)
        kw = (kT_h[h] * wk_row).astype(BF)
        state_ref[h] = jnp.concatenate([decay, decay], axis=1) * st[h] + _dot(kw, v_ext[h])
        m_ref[h] = m_new


def _mlstm(proj, gates, B, S):
    N = proj.shape[0]
    L = MLSTM_CHUNK
    nc = S // L
    H = MLSTM_HEADS
    t = np.arange(L)
    lower = (t[None, :] <= t[:, None]).astype(np.float32)
    tri = jnp.asarray(np.stack([lower, lower.T]), dtype=F32)
    sel_np = np.zeros((H, LANES, LANES), np.float32)
    for h in range(H):
        sel_np[h, H + h, :] = 1.0
    sel = jnp.asarray(sel_np, dtype=BF)

    def row(b, d, c):
        return b * nc + c + d * (nc - 1 - 2 * c)

    return pl.pallas_call(
        _mlstm_kernel,
        grid=(B, 2, nc),
        in_specs=[
            pl.BlockSpec((L, H * MLSTM_DK), lambda b, d, c: (row(b, d, c), 0)),
            pl.BlockSpec((L, H * MLSTM_DK), lambda b, d, c: (row(b, d, c), 1)),
            pl.BlockSpec((L, H * MLSTM_DV), lambda b, d, c: (row(b, d, c), 1)),
            pl.BlockSpec((L, LANES), lambda b, d, c: (row(b, d, c), 0)),
            pl.BlockSpec((2, L, L), lambda b, d, c: (0, 0, 0)),
            pl.BlockSpec((H, LANES, LANES), lambda b, d, c: (0, 0, 0)),
        ],
        out_specs=pl.BlockSpec((None, L, H * MLSTM_DV), lambda b, d, c: (d, row(b, d, c), 0)),
        out_shape=jax.ShapeDtypeStruct((2, N, H * MLSTM_DV), BF),
        scratch_shapes=[
            pltpu.VMEM((H, 2 * MLSTM_DK, 2 * LANES), F32),
            pltpu.VMEM((H, 1, LANES), F32),
        ],
        compiler_params=_cp(("arbitrary", "arbitrary", "arbitrary")),
        name="mlstm",
    )(proj, proj, proj, gates, tri, sel)


def _outproj_tail(a, x_ref, g1_ref, w_ref, n2_ref, sc2_ref, sh2_ref, wr_ref, xo_ref, h2_ref, lg_ref):
    xn = x_ref[...] + g1_ref[...] * _dot(a, w_ref[...])
    xo_ref[...] = xn
    h2 = _norm_mod(xn, n2_ref[...], sc2_ref[...], sh2_ref[...]).astype(BF)
    h2_ref[...] = h2
    lg = _dot_nt(wr_ref[...], h2)
    for cb in range(lg.shape[1] // LANES):
        lg_ref[cb] = lg[:, cb * LANES:(cb + 1) * LANES]


def _outproj_attn_kernel(a_ref, x_ref, g1_ref, w_ref, n2_ref, sc2_ref, sh2_ref, wr_ref,
                         xo_ref, h2_ref, lg_ref):
    _outproj_tail(a_ref[...], x_ref, g1_ref, w_ref, n2_ref, sc2_ref, sh2_ref, wr_ref,
                  xo_ref, h2_ref, lg_ref)


def _outproj_mlstm_kernel(hf_ref, hb_ref, og_ref, on_ref, bd_ref, x_ref, g1_ref, w_ref, n2_ref,
                          sc2_ref, sh2_ref, wr_ref, xo_ref, h2_ref, lg_ref, a_ref):
    for j in range(D_MODEL // 256):
        sl = slice(j * 256, (j + 1) * 256)
        hs = hf_ref[:, sl].astype(F32) + hb_ref[:, sl].astype(F32)
        ss = _group_sumsq(hs, bd_ref[...])
        hn = hs * lax.rsqrt(ss * (1.0 / MLSTM_DV) + EPS) * on_ref[:, sl]
        a_ref[:, sl] = (hn * _sigmoid(og_ref[:, sl].astype(F32))).astype(BF)
    _outproj_tail(a_ref[...], x_ref, g1_ref, w_ref, n2_ref, sc2_ref, sh2_ref, wr_ref,
                  xo_ref, h2_ref, lg_ref)


def _outproj(kind, mix_inputs, x2, S, g1, w_bf, n2, sc2, sh2, wrT_bf, tm=512):
    N = x2.shape[0]
    spb = S // tm
    row = lambda i: (i, 0)
    const = lambda i: (0, 0)
    per_b = lambda i: (i // spb, 0, 0)
    tail_specs = [
        pl.BlockSpec((tm, D_MODEL), row),
        pl.BlockSpec((None, 1, D_MODEL), per_b),
        pl.BlockSpec((D_MODEL, D_MODEL), const),
        pl.BlockSpec((1, D_MODEL), const),
        pl.BlockSpec((None, 1, D_MODEL), per_b),
        pl.BlockSpec((None, 1, D_MODEL), per_b),
        pl.BlockSpec((N_EXPERTS, D_MODEL), const),
    ]
    out_specs = [
        pl.BlockSpec((tm, D_MODEL), row),
        pl.BlockSpec((tm, D_MODEL), row),
        pl.BlockSpec((tm // LANES, N_EXPERTS, LANES), lambda i: (i, 0, 0)),
    ]
    out_shape = [
        jax.ShapeDtypeStruct((N, D_MODEL), F32),
        jax.ShapeDtypeStruct((N, D_MODEL), BF),
        jax.ShapeDtypeStruct((N // LANES, N_EXPERTS, LANES), F32),
    ]
    tail_args = (x2, g1, w_bf, n2, sc2, sh2, wrT_bf)
    if kind == "attn":
        (a,) = mix_inputs
        return pl.pallas_call(
            _outproj_attn_kernel,
            grid=(N // tm,),
            in_specs=[pl.BlockSpec((tm, D_MODEL), row)] + tail_specs,
            out_specs=out_specs, out_shape=out_shape,
            compiler_params=_cp(("arbitrary",)),
            name="outproj_attn",
        )(a, *tail_args)
    hfb, proj, onorm_row = mix_inputs
    bd = _block_diag_ones(MLSTM_DV)
    return pl.pallas_call(
        _outproj_mlstm_kernel,
        grid=(N // tm,),
        in_specs=[
            pl.BlockSpec((None, tm, D_MODEL), lambda i: (0, i, 0)),
            pl.BlockSpec((None, tm, D_MODEL), lambda i: (1, i, 0)),
            pl.BlockSpec((tm, D_MODEL), lambda i: (i, 2)),
            pl.BlockSpec((1, D_MODEL), const),
            pl.BlockSpec((256, 256), const),
        ] + tail_specs,
        out_specs=out_specs, out_shape=out_shape,
        scratch_shapes=[pltpu.VMEM((tm, D_MODEL), BF)],
        compiler_params=_cp(("arbitrary",)),
        name="outproj_mlstm",
    )(hfb, hfb, proj, onorm_row, bd, *tail_args)


def _router_kernel(lg_ref, triu_ref, rank_ref, gate_ref, offs_ref, bits_ref, *, cap, blocks_per_tb):
    NB = lg_ref.shape[0]
    E = N_EXPERTS
    shape = (E, LANES)

    def softmax_body(b, carry):
        l = lg_ref[b]
        e = jnp.exp(l - jnp.max(l, axis=0, keepdims=True))
        aff = e / jnp.sum(e, axis=0, keepdims=True)
        gate_ref[b] = aff
        bits_ref[b] = pltpu.bitcast(aff, jnp.int32)
        return carry

    lax.fori_loop(0, NB, softmax_body, 0)

    def count(pred):
        def body(b, acc):
            return acc + jnp.where(pred(bits_ref[b]), 1.0, 0.0)
        acc = lax.fori_loop(0, NB, body, jnp.zeros(shape, F32))
        return jnp.broadcast_to(jnp.sum(acc, axis=1, keepdims=True), shape)

    def search_body(i, T):
        cand = T | jnp.left_shift(jnp.int32(1), 30 - i)
        cnt = count(lambda x: x >= cand)
        return jnp.where(cnt >= cap, cand, T)

    T = lax.fori_loop(0, 31, search_body, jnp.zeros(shape, jnp.int32))
    need = cap - count(lambda x: x > T)

    triu = triu_ref[...]
    ones = jnp.ones((LANES, LANES), BF)

    def tb_body(tb, carry):
        ceq, csel = carry
        offs_ref[tb] = csel
        for k in range(blocks_per_tb):
            b = tb * blocks_per_tb + k
            x = bits_ref[b]
            gt = x > T
            eq = x == T
            eqf = jnp.where(eq, 1.0, 0.0)
            eqb = eqf.astype(BF)
            rank_eq = ceq + _dot(eqb, triu) - eqf
            sel = gt | (eq & (rank_eq < need))
            self_ = jnp.where(sel, 1.0, 0.0)
            selb = self_.astype(BF)
            rank = csel + _dot(selb, triu) - self_
            rank_ref[b] = jnp.where(sel, rank, -1.0).astype(jnp.int32)
            gate_ref[b] = jnp.where(sel, gate_ref[b], 0.0)
            ceq = ceq + _dot(eqb, ones)
            csel = csel + _dot(selb, ones)
        return ceq, csel

    n_tb = NB // blocks_per_tb
    _, csel = lax.fori_loop(0, n_tb, tb_body, (jnp.zeros(shape, F32), jnp.zeros(shape, F32)))
    offs_ref[n_tb] = csel


def _router(lg, cap, tb_tokens):
    NB = lg.shape[0]
    bpt = tb_tokens // LANES
    n_tb = NB // bpt
    u = np.arange(LANES)
    triu = jnp.asarray((u[:, None] <= u[None, :]).astype(np.float32), dtype=BF)
    kern = functools.partial(_router_kernel, cap=float(cap), blocks_per_tb=bpt)
    return pl.pallas_call(
        kern,
        out_shape=[
            jax.ShapeDtypeStruct((NB, N_EXPERTS, LANES), jnp.int32),
            jax.ShapeDtypeStruct((NB, N_EXPERTS, LANES), F32),
            jax.ShapeDtypeStruct((n_tb + 1, N_EXPERTS, LANES), F32),
        ],
        scratch_shapes=[pltpu.VMEM((NB, N_EXPERTS, LANES), jnp.int32)],
        compiler_params=pltpu.CompilerParams(vmem_limit_bytes=VMEM_LIMIT),
        name="router",
    )(lg, triu)


def _schedule(offs, cap, slot_tile, token_major):
    E = N_EXPERTS
    n_t = offs.shape[0] - 1
    n_s = cap // slot_tile
    lo = offs[:-1]
    hi = offs[1:]
    nonempty = hi > lo
    s_lo = jnp.minimum(lo // slot_tile, n_s - 1)
    s_hi = jnp.where(nonempty, (hi - 1) // slot_tile, s_lo)
    cnt = jnp.where(nonempty, s_hi - s_lo + 1, 0)
    if token_major:
        cnt = cnt.at[:, 0].set(jnp.maximum(cnt[:, 0], 1))
        cnt_flat, slo_flat = cnt.reshape(-1), s_lo.reshape(-1)
        n_steps = E * (n_s + n_t) + n_t
    else:
        cnt_flat, slo_flat = cnt.T.reshape(-1), s_lo.T.reshape(-1)
        n_steps = E * (n_s + n_t)
    ends = jnp.cumsum(cnt_flat)
    total = ends[-1]
    i = jnp.arange(n_steps, dtype=jnp.int32)
    ic = jnp.minimum(i, total - 1)
    before = ends[None, :-1] <= ic[:, None]
    pair = jnp.sum(before, axis=1).astype(jnp.int32)
    val = slo_flat - (ends - cnt_flat)
    sv = val[0] + jnp.sum(jnp.where(before, (val[1:] - val[:-1])[None, :], 0), axis=1)
    s = (ic + sv).astype(jnp.int32)
    if token_major:
        t, e = pair // E, pair % E
        key = t
    else:
        e, t = pair // n_t, pair % n_t
        key = e * n_s + s
    valid = i < total
    change = key[1:] != key[:-1]
    first = valid & jnp.concatenate([jnp.ones((1,), bool), change])
    last = valid & jnp.concatenate([change | ~valid[1:], jnp.ones((1,), bool)])
    flags = first.astype(jnp.int32) + 2 * last.astype(jnp.int32) + 4 * valid.astype(jnp.int32)
    return e.astype(jnp.int32), s, t.astype(jnp.int32), flags


def _one_hot_slots(rank_ref, e, slot0, T, blocks):
    slot = slot0 + lax.broadcasted_iota(jnp.int32, (T, LANES), 0)
    pieces = [jnp.where(rank_ref[a, pl.ds(e, 1), :] == slot, 1.0, 0.0).astype(BF) for a in blocks]
    return jnp.concatenate(pieces, axis=1)


def _ffn_kernel(se_ref, ss_ref, st_ref, fl_ref, offs_ref, x_ref, rank_ref, gate_ref, wg_ref, wu_ref,
                wd_ref, ye_ref, xacc_ref, gacc_ref, *, T, sub, n_sub, f_chunk):
    i = pl.program_id(0)
    e = se_ref[i]
    s = ss_ref[i]
    fl = fl_ref[i]
    bps = sub // LANES

    @pl.when((fl & 1) != 0)
    def _():
        xacc_ref[...] = jnp.zeros(xacc_ref.shape, F32)
        gacc_ref[...] = jnp.zeros(gacc_ref.shape, F32)

    for a in range(n_sub):
        tsub = st_ref[i] * n_sub + a
        lo = offs_ref[tsub * N_EXPERTS + e]
        hi = offs_ref[(tsub + 1) * N_EXPERTS + e]

        @pl.when(((fl & 4) != 0) & (hi > s * T) & (lo < (s + 1) * T) & (hi > lo))
        def _(a=a):
            blocks = range(a * bps, (a + 1) * bps)
            P = _one_hot_slots(rank_ref, e, s * T, T, blocks)
            xacc_ref[...] += _dot(P, x_ref[a * sub:(a + 1) * sub, :])
            g = jnp.concatenate([gate_ref[b, pl.ds(e, 1), :] for b in blocks], axis=1)
            parts = [t.astype(F32) for t in _split3(g)]
            grows = jnp.concatenate(parts + [jnp.zeros((13, sub), F32)], axis=0).astype(BF)
            gacc_ref[...] += _dot_nt(P, grows)

    @pl.when((fl & 2) != 0)
    def _():
        x = xacc_ref[...].astype(BF)
        y = jnp.zeros((T, D_MODEL), F32)
        for c in range(EXPERT_FF // f_chunk):
            sl = slice(c * f_chunk, (c + 1) * f_chunk)
            gt = _dot(x, wg_ref[:, sl])
            up = _dot(x, wu_ref[:, sl])
            hid = (gt * _sigmoid(gt) * up).astype(BF)
            y = y + _dot(hid, wd_ref[sl, :])
        ga = gacc_ref[...]
        gcol = ga[:, 0:1] + ga[:, 1:2] + ga[:, 2:3]
        ye_ref[...] = (y * gcol).astype(ye_ref.dtype)


def _ffn(sched, offs_flat, h2, rankm, gate, wg, wu, wd, cap, T, sub, n_sub):
    se, ss, st, fl = sched
    n_steps = se.shape[0]
    TB = sub * n_sub
    nb = TB // LANES
    kern = functools.partial(_ffn_kernel, T=T, sub=sub, n_sub=n_sub, f_chunk=512)
    grid_spec = pltpu.PrefetchScalarGridSpec(
        num_scalar_prefetch=5,
        grid=(n_steps,),
        in_specs=[
            pl.BlockSpec((TB, D_MODEL), lambda i, se, ss, st, fl, of: (st[i], 0)),
            pl.BlockSpec((nb, N_EXPERTS, LANES), lambda i, se, ss, st, fl, of: (st[i], 0, 0)),
            pl.BlockSpec((nb, N_EXPERTS, LANES), lambda i, se, ss, st, fl, of: (st[i], 0, 0)),
            pl.BlockSpec((None, D_MODEL, EXPERT_FF), lambda i, se, ss, st, fl, of: (se[i], 0, 0)),
            pl.BlockSpec((None, D_MODEL, EXPERT_FF), lambda i, se, ss, st, fl, of: (se[i], 0, 0)),
            pl.BlockSpec((None, EXPERT_FF, D_MODEL), lambda i, se, ss, st, fl, of: (se[i], 0, 0)),
        ],
        out_specs=pl.BlockSpec((None, T, D_MODEL), lambda i, se, ss, st, fl, of: (se[i], ss[i], 0)),
        scratch_shapes=[pltpu.VMEM((T, D_MODEL), F32), pltpu.VMEM((T, 16), F32)],
    )
    return pl.pallas_call(
        kern,
        grid_spec=grid_spec,
        out_shape=jax.ShapeDtypeStruct((N_EXPERTS, cap, D_MODEL), BF),
        compiler_params=_cp(("arbitrary",)),
        name="moe_ffn",
    )(se, ss, st, fl, offs_flat, h2, rankm, gate, wg, wu, wd)


def _combine_kernel(se_ref, ss_ref, st_ref, fl_ref, offs_ref, ye_ref, rank_ref, x_ref, g2_ref, o_ref,
                    acc_ref, *, T, n_tiles, n_blocks, subs_per_tile):
    i = pl.program_id(0)
    e = se_ref[i]
    s = ss_ref[i]
    t = st_ref[i]
    fl = fl_ref[i]

    @pl.when((fl & 1) != 0)
    def _():
        acc_ref[...] = jnp.zeros(acc_ref.shape, F32)

    lo = offs_ref[t * subs_per_tile * N_EXPERTS + e]
    hi = offs_ref[(t + 1) * subs_per_tile * N_EXPERTS + e]
    for j in range(n_tiles):
        slot0 = (s * n_tiles + j) * T

        @pl.when(((fl & 4) != 0) & (hi > slot0) & (lo < slot0 + T) & (hi > lo))
        def _(j=j, slot0=slot0):
            P = _one_hot_slots(rank_ref, e, slot0, T, range(n_blocks))
            acc_ref[...] += _dot_tn(P, ye_ref[j * T:(j + 1) * T, :])

    @pl.when((fl & 2) != 0)
    def _():
        o_ref[...] = x_ref[...] + g2_ref[...] * acc_ref[...]


def _combine(sched, offs_flat, ye, rankm, x1, g2, S, T, n_tiles, TT, sub):
    se, ss, st, fl = sched
    n_steps = se.shape[0]
    N = x1.shape[0]
    nb = TT // LANES
    spb = S // TT
    YB = T * n_tiles
    kern = functools.partial(_combine_kernel, T=T, n_tiles=n_tiles, n_blocks=nb, subs_per_tile=TT // sub)
    grid_spec = pltpu.PrefetchScalarGridSpec(
        num_scalar_prefetch=5,
        grid=(n_steps,),
        in_specs=[
            pl.BlockSpec((None, YB, D_MODEL), lambda i, se, ss, st, fl, of: (se[i], ss[i], 0)),
            pl.BlockSpec((nb, N_EXPERTS, LANES), lambda i, se, ss, st, fl, of: (st[i], 0, 0)),
            pl.BlockSpec((TT, D_MODEL), lambda i, se, ss, st, fl, of: (st[i], 0)),
            pl.BlockSpec((None, 1, D_MODEL), lambda i, se, ss, st, fl, of: (st[i] // spb, 0, 0)),
        ],
        out_specs=pl.BlockSpec((TT, D_MODEL), lambda i, se, ss, st, fl, of: (st[i], 0)),
        scratch_shapes=[pltpu.VMEM((TT, D_MODEL), F32)],
    )
    return pl.pallas_call(
        kern,
        grid_spec=grid_spec,
        out_shape=jax.ShapeDtypeStruct((N, D_MODEL), F32),
        compiler_params=_cp(("arbitrary",)),
        name="moe_combine",
    )(se, ss, st, fl, offs_flat, ye, rankm, x1, g2)


MOE_T = 256
MOE_SUB = 512
FFN_SUBS = 4
COMBINE_TT = 1024
YE_TILES = 2


def _moe(h2, lg, x1, g2, S, wg, wu, wd):
    N = x1.shape[0]
    cap = (CAPACITY_FACTOR * N) // N_EXPERTS
    T = min(MOE_T, cap)
    sub = min(MOE_SUB, S)
    n_sub = min(FFN_SUBS, S // sub)
    TT = min(COMBINE_TT, S)
    n_tiles = min(YE_TILES, cap // T)
    rankm, gate, offs = _router(lg, cap, sub)
    offs_i = offs[:, :, 0].astype(jnp.int32)
    offs_flat = offs_i.reshape(-1)
    fsched = _schedule(offs_i[::n_sub], cap, T, token_major=False)
    csched = _schedule(offs_i[::TT // sub], cap, T * n_tiles, token_major=True)
    ye = _ffn(fsched, offs_flat, h2, rankm, gate, wg, wu, wd, cap, T, sub, n_sub)
    return _combine(csched, offs_flat, ye, rankm, x1, g2, S, T, n_tiles, TT, sub)


def _trunk(x, mods, P):
    B, S, _ = x.shape
    x2 = x.reshape(B * S, D_MODEL)
    for l in range(DEPTH):
        mod = mods[l]
        sh1, sc1, g1, sh2, sc2, g2 = [mod[:, j][:, None, :] for j in range(N_MOD)]
        n1 = P["norm_g"][l, 0][None, :]
        n2 = P["norm_g"][l, 1][None, :]
        j = l // 2
        if l % 2 == 0:
            lambda_init = 0.8 - 0.6 * math.exp(-0.3 * l)
            qkv = _inproj_attn(x2, S, n1, sc1, sh1, P["attn_w_in"][j], P["attn_gain"][j])
            a = _attention(qkv, B, S, P["slopes"], P["attn_dmax"][j], P["attn_lambda"][j],
                           P["attn_subln"][j][None, :], lambda_init)
            x1, h2, lg = _outproj("attn", (a,), x2, S, g1, P["attn_w_out"][j], n2, sc2, sh2,
                                  P["w_routerT"][l])
        else:
            proj, gates = _inproj_mlstm(x2, S, n1, sc1, sh1, P["mlstm_w_main"][j], P["mlstm_w_gate"][j],
                                        P["mlstm_b_gate"][j])
            hfb = _mlstm(proj, gates, B, S)
            x1, h2, lg = _outproj("mlstm", (hfb, proj, P["mlstm_out_norm"][j]), x2, S, g1,
                                  P["mlstm_w_out"][j], n2, sc2, sh2, P["w_routerT"][l])
        x2 = _moe(h2, lg, x1, g2, S, P["w_exp_gate"][l], P["w_exp_up"][l], P["w_exp_down"][l])
    return x2.reshape(B, S, D_MODEL)


def _prepare(norm_g, attn_w_in, attn_q_gain, attn_k_gain, attn_lambda, attn_subln, attn_w_out,
             mlstm_w_in, mlstm_b_gate, mlstm_out_norm, mlstm_w_out, w_router, w_exp_gate, w_exp_up,
             w_exp_down):
    H = ATTN_HEADS
    n_main = 2 * MLSTM_HEADS * MLSTM_DK + MLSTM_HEADS * MLSTM_DV + D_MODEL
    n_gate = 4 * MLSTM_HEADS
    qg = jnp.tile(attn_q_gain * (ATTN_HEAD_DIM ** -0.5 * _LOG2E), (1, 2 * H))
    kg = jnp.tile(attn_k_gain, (1, 2 * H))
    slopes = 2.0 ** (-8.0 * np.arange(1, H + 1) / H)
    assert all(np.frexp(slopes)[0] == 0.5), "ALiBi slopes must be powers of two for bf16-exact features"
    smax = (8.0 * _LOG2E * 1.02) * jnp.max(jnp.abs(attn_q_gain), axis=1) * jnp.max(jnp.abs(attn_k_gain), axis=1)
    dmax = jnp.ceil((152.0 + 2.0 * smax[:, None]) / jnp.asarray(slopes * _LOG2E, F32)[None, :])
    dmax = jnp.clip(dmax, 0.0, 2.0 ** 30).astype(jnp.int32)
    return {
        "attn_dmax": dmax,
        "norm_g": norm_g,
        "attn_w_in": attn_w_in.astype(BF),
        "attn_gain": jnp.concatenate([qg, kg], axis=1)[:, None, :],
        "attn_lambda": attn_lambda,
        "attn_subln": attn_subln,
        "attn_w_out": attn_w_out.astype(BF),
        "slopes": jnp.asarray(slopes, F32),
        "mlstm_w_main": mlstm_w_in[:, :, :n_main].astype(BF),
        "mlstm_w_gate": jnp.pad(mlstm_w_in[:, :, n_main:], ((0, 0), (0, 0), (0, LANES - n_gate))).astype(BF),
        "mlstm_b_gate": jnp.pad(mlstm_b_gate, ((0, 0), (0, LANES - n_gate)))[:, None, :],
        "mlstm_out_norm": jnp.tile(mlstm_out_norm, (1, MLSTM_HEADS))[:, None, :],
        "mlstm_w_out": mlstm_w_out.astype(BF),
        "w_routerT": jnp.swapaxes(w_router, 1, 2).astype(BF),
        "w_exp_gate": w_exp_gate.astype(BF),
        "w_exp_up": w_exp_up.astype(BF),
        "w_exp_down": w_exp_down.astype(BF),
    }


def kernel(x_prompt, x_sample, c_prompt, c_sample, norm_g, w_ada, b_ada, attn_w_in, attn_q_gain,
           attn_k_gain, attn_lambda, attn_subln, attn_w_out, mlstm_w_in, mlstm_b_gate, mlstm_out_norm,
           mlstm_w_out, w_router, w_exp_gate, w_exp_up, w_exp_down):
    P = _prepare(norm_g, attn_w_in, attn_q_gain, attn_k_gain, attn_lambda, attn_subln, attn_w_out,
                 mlstm_w_in, mlstm_b_gate, mlstm_out_norm, mlstm_w_out, w_router, w_exp_gate,
                 w_exp_up, w_exp_down)
    bp, bs = c_prompt.shape[0], c_sample.shape[0]
    pad = (-(bp + bs)) % 8
    c_all = jnp.concatenate([c_prompt, c_sample, jnp.zeros((pad, D_MODEL), F32)], axis=0)
    mods = _ada(c_all, w_ada, b_ada)
    mods = mods.reshape(DEPTH, c_all.shape[0], N_MOD, D_MODEL)
    y_prompt = _trunk(x_prompt, mods[:, :bp], P)
    y_sample = _trunk(x_sample, mods[:, bp:bp + bs], P)
    return (y_prompt, y_sample)
```

```python
import functools
import math

import jax
import jax.numpy as jnp
import numpy as np
from jax import lax
from jax.experimental import pallas as pl
from jax.experimental.pallas import tpu as pltpu

D_MODEL = 1024
DEPTH = 4
ATTN_HEADS = 8
ATTN_HEAD_DIM = 64
MLSTM_HEADS = 8
MLSTM_DK = 64
MLSTM_DV = 128
MLSTM_CHUNK = 128
N_EXPERTS = 16
CAPACITY_FACTOR = 2
EXPERT_FF = 2048
N_MOD = 6
EPS = 1e-6

LANES = 128
VMEM_LIMIT = 56 * 1024 * 1024

BF = jnp.bfloat16
F32 = jnp.float32
NEG = -1e30


def _cp(sem, vmem=VMEM_LIMIT):
    return pltpu.CompilerParams(dimension_semantics=sem, vmem_limit_bytes=vmem)


def _sigmoid(x):
    return 1.0 / (1.0 + jnp.exp(-x))


def _dot(a, b):
    return jnp.dot(a, b, preferred_element_type=F32)


def _dot_nt(a, b):
    return lax.dot_general(a, b, (((1,), (1,)), ((), ())), preferred_element_type=F32)


def _dot_tn(a, b):
    return lax.dot_general(a, b, (((0,), (0,)), ((), ())), preferred_element_type=F32)


def _split3(x):
    a = x.astype(BF)
    r = x - a.astype(F32)
    b = r.astype(BF)
    c = (r - b.astype(F32)).astype(BF)
    return a, b, c


def _ada_kernel(c_ref, w_ref, b_ref, o_ref):
    c = c_ref[...]
    s = (c * _sigmoid(c)).astype(BF)
    o_ref[...] = _dot(s, w_ref[...].astype(BF)) + b_ref[...]


def _ada(c_all, w_ada, b_ada):
    bp = c_all.shape[0]
    n_out = N_MOD * D_MODEL
    tn = 1536
    return pl.pallas_call(
        _ada_kernel,
        grid=(DEPTH, n_out // tn),
        in_specs=[
            pl.BlockSpec((bp, D_MODEL), lambda l, j: (0, 0)),
            pl.BlockSpec((None, D_MODEL, tn), lambda l, j: (l, 0, j)),
            pl.BlockSpec((None, 1, tn), lambda l, j: (l, 0, j)),
        ],
        out_specs=pl.BlockSpec((None, bp, tn), lambda l, j: (l, 0, j)),
        out_shape=jax.ShapeDtypeStruct((DEPTH, bp, n_out), F32),
        compiler_params=_cp(("arbitrary", "arbitrary")),
        name="ada",
    )(c_all, w_ada, b_ada.reshape(DEPTH, 1, n_out))


def _norm_mod(x, g, sc, sh):
    r = lax.rsqrt(jnp.mean(x * x, axis=-1, keepdims=True) + EPS)
    return x * r * g * (1.0 + sc) + sh


def _group_sumsq(y, bd):
    return _dot((y * y).astype(BF), bd)


def _block_diag_ones(group, size=256):
    i = np.arange(size) // group
    return jnp.asarray((i[:, None] == i[None, :]).astype(np.float32), dtype=BF)


def _inproj_attn_kernel(x_ref, g_ref, sc_ref, sh_ref, w_ref, gain_ref, bd_ref, o_ref):
    h = _norm_mod(x_ref[...], g_ref[...], sc_ref[...], sh_ref[...]).astype(BF)
    n_chunks = o_ref.shape[1] // 256
    n_norm = gain_ref.shape[1] // 256
    for j in range(n_chunks):
        sl = slice(j * 256, (j + 1) * 256)
        y = _dot(h, w_ref[:, sl])
        if j < n_norm:
            ss = _group_sumsq(y, bd_ref[...])
            y = y * lax.rsqrt(ss * (1.0 / ATTN_HEAD_DIM) + EPS) * gain_ref[:, sl]
        o_ref[:, sl] = y.astype(o_ref.dtype)


def _inproj_attn(x2, S, g, sc, sh, w_bf, gain_row, tm=512):
    N = x2.shape[0]
    n_out = w_bf.shape[1]
    spb = S // tm
    bd = _block_diag_ones(ATTN_HEAD_DIM)
    return pl.pallas_call(
        _inproj_attn_kernel,
        grid=(N // tm,),
        in_specs=[
            pl.BlockSpec((tm, D_MODEL), lambda i: (i, 0)),
            pl.BlockSpec((1, D_MODEL), lambda i: (0, 0)),
            pl.BlockSpec((None, 1, D_MODEL), lambda i: (i // spb, 0, 0)),
            pl.BlockSpec((None, 1, D_MODEL), lambda i: (i // spb, 0, 0)),
            pl.BlockSpec((D_MODEL, n_out), lambda i: (0, 0)),
            pl.BlockSpec((1, gain_row.shape[1]), lambda i: (0, 0)),
            pl.BlockSpec((256, 256), lambda i: (0, 0)),
        ],
        out_specs=pl.BlockSpec((tm, n_out), lambda i: (i, 0)),
        out_shape=jax.ShapeDtypeStruct((N, n_out), BF),
        compiler_params=_cp(("arbitrary",)),
        name="inproj_attn",
    )(x2, g, sc, sh, w_bf, gain_row, bd)


def _inproj_mlstm_kernel(x_ref, g_ref, sc_ref, sh_ref, w_ref, wg_ref, bg_ref, o_ref, og_ref):
    h = _norm_mod(x_ref[...], g_ref[...], sc_ref[...], sh_ref[...]).astype(BF)
    hk = MLSTM_HEADS * MLSTM_DK
    for j in range(o_ref.shape[1] // 256):
        sl = slice(j * 256, (j + 1) * 256)
        y = _dot(h, w_ref[:, sl])
        if hk <= j * 256 < 2 * hk:
            y = y * (MLSTM_DK ** -0.5)
        o_ref[:, sl] = y.astype(o_ref.dtype)
    og_ref[...] = _dot(h, wg_ref[...]) + bg_ref[...]


def _inproj_mlstm(x2, S, g, sc, sh, w_bf, wg_bf, bg_row, tm=512):
    N = x2.shape[0]
    n_out = w_bf.shape[1]
    spb = S // tm
    return pl.pallas_call(
        _inproj_mlstm_kernel,
        grid=(N // tm,),
        in_specs=[
            pl.BlockSpec((tm, D_MODEL), lambda i: (i, 0)),
            pl.BlockSpec((1, D_MODEL), lambda i: (0, 0)),
            pl.BlockSpec((None, 1, D_MODEL), lambda i: (i // spb, 0, 0)),
            pl.BlockSpec((None, 1, D_MODEL), lambda i: (i // spb, 0, 0)),
            pl.BlockSpec((D_MODEL, n_out), lambda i: (0, 0)),
            pl.BlockSpec((D_MODEL, LANES), lambda i: (0, 0)),
            pl.BlockSpec((1, LANES), lambda i: (0, 0)),
        ],
        out_specs=[
            pl.BlockSpec((tm, n_out), lambda i: (i, 0)),
            pl.BlockSpec((tm, LANES), lambda i: (i, 0)),
        ],
        out_shape=[
            jax.ShapeDtypeStruct((N, n_out), BF),
            jax.ShapeDtypeStruct((N, LANES), F32),
        ],
        compiler_params=_cp(("arbitrary",)),
        name="inproj_mlstm",
    )(x2, g, sc, sh, w_bf, wg_bf, bg_row)


_LOG2E = math.log2(math.e)
N_FEAT = 12


def _log2e_split():
    c = np.float32(_LOG2E)
    out = []
    for _ in range(3):
        t = np.asarray(c, dtype=BF).astype(np.float32)
        out.append(float(t))
        c = np.float32(c - t)
    return out


def _attn_tables(S):
    c = _log2e_split()
    pos = np.arange(S)
    kf = np.zeros((S, LANES), np.float32)
    qc = np.zeros((1, LANES), np.float32)
    for t in range(3):
        kf[:, t] = -128.0 * c[t]
        kf[:, 3 + t] = -c[t]
        kf[:, 6 + t] = pos // 128
        kf[:, 9 + t] = pos % 128
        qc[0, 6 + t] = 128.0 * c[t]
        qc[0, 9 + t] = c[t]
    return jnp.asarray(kf, dtype=BF), jnp.asarray(qc, dtype=F32)


def _attn_kernel(slopes_ref, dmax_ref, q_ref, k_ref, v_ref, kf_ref, qc_ref, lam_ref, subln_ref, o_ref,
                 qs_ref, m_ref, l_ref, acc_ref, *, tq, tk, S, lambda_init):
    h = pl.program_id(1)
    qi = pl.program_id(2)
    nk = S // tk
    slope = slopes_ref[h]
    r0 = qi * tq

    q = q_ref[...]
    lane = lax.broadcasted_iota(jnp.int32, (tq, LANES), 1)
    zero = jnp.zeros_like(q)
    q0 = jnp.where(lane < ATTN_HEAD_DIM, q, zero)
    q1 = jnp.where(lane >= ATTN_HEAD_DIM, q, zero)
    pos = r0 + lax.broadcasted_iota(jnp.int32, (tq, LANES), 0)
    hi = (pos >> 7).astype(F32)
    lo = (pos & 127).astype(F32)
    feat = jnp.where(lane < 3, hi, jnp.where(lane < 6, lo, qc_ref[...])) * slope
    for sgn, f in ((0, feat.astype(BF)), (1, (-feat).astype(BF))):
        qs_ref[sgn, 0:tq, 0:LANES] = q0
        qs_ref[sgn, tq:2 * tq, 0:LANES] = q1
        qs_ref[sgn, 0:tq, LANES:2 * LANES] = f
        qs_ref[sgn, tq:2 * tq, LANES:2 * LANES] = f

    def scores(ki, sgn):
        off = pl.multiple_of(ki * tk, tk)
        kaug = jnp.concatenate([k_ref[pl.ds(off, tk), :], kf_ref[pl.ds(off, tk), :]], axis=1)
        return _dot_nt(qs_ref[sgn], kaug), off

    def diag_correction(off):
        row = lax.broadcasted_iota(jnp.int32, (tq, tk), 0)
        col = lax.broadcasted_iota(jnp.int32, (tq, tk), 1)
        corr = jnp.minimum(row - col + (r0 - off), 0).astype(F32) * (2.0 * _LOG2E * slope)
        return jnp.concatenate([corr, corr], axis=0)

    def chunk(ki, sgn, state, diagonal=False):
        m_prev, l_prev, acc_prev = state
        s, off = scores(ki, sgn)
        if diagonal:
            s = s + diag_correction(off)
        m_new = jnp.maximum(m_prev, jnp.max(s, axis=1, keepdims=True))
        p = jnp.exp2(s - jnp.concatenate([m_new] * (tk // LANES), axis=1))
        alpha = jnp.exp2(m_prev - m_new)
        l_new = alpha * l_prev + jnp.sum(p, axis=1, keepdims=True)
        acc_new = alpha * acc_prev + _dot(p.astype(BF), v_ref[pl.ds(off, tk), :])
        return m_new, l_new, acc_new

    def load_state():
        return m_ref[...], l_ref[...], acc_ref[...]

    def store_state(state):
        m_ref[...], l_ref[...], acc_ref[...] = state

    n_diag = max(1, tq // tk)
    kd = r0 // tk
    s, off = scores(kd, 0)
    s = s + diag_correction(off)
    m0 = jnp.max(s, axis=1, keepdims=True)
    p = jnp.exp2(s - m0)
    state = (jnp.broadcast_to(m0, m_ref.shape),
             jnp.broadcast_to(jnp.sum(p, axis=1, keepdims=True), l_ref.shape),
             _dot(p.astype(BF), v_ref[pl.ds(off, tk), :]))
    for j in range(1, n_diag):
        state = chunk(kd + j, 0, state, diagonal=True)
    store_state(state)

    dmax = dmax_ref[h]
    k_lo = jnp.maximum(r0 - dmax, 0) // tk
    k_hi = jnp.minimum((r0 + tq - 1 + dmax) // tk, nk - 1)
    n_before = kd - k_lo
    n_off = n_before + (k_hi - (kd + n_diag - 1))

    def nth(i):
        after = (i >= n_before).astype(jnp.int32)
        return k_lo + i + after * n_diag, after

    def pair_body(j, c):
        state = load_state()
        state = chunk(*nth(2 * j), state)
        state = chunk(*nth(2 * j + 1), state)
        store_state(state)
        return c

    lax.fori_loop(0, n_off // 2, pair_body, 0)

    @pl.when(n_off % 2 == 1)
    def _():
        store_state(chunk(*nth(n_off - 1), load_state()))


    lam = lam_ref[...]
    l01 = jnp.sum(lam[0:1, :] * lam[1:2, :], axis=1, keepdims=True)
    l23 = jnp.sum(lam[2:3, :] * lam[3:4, :], axis=1, keepdims=True)
    lmbda = jnp.exp(l01) - jnp.exp(l23) + lambda_init
    o0 = acc_ref[0:tq, :] / l_ref[0:tq, :]
    o1 = acc_ref[tq:2 * tq, :] / l_ref[tq:2 * tq, :]
    o = o0 - lmbda * o1
    r = lax.rsqrt(jnp.mean(o * o, axis=-1, keepdims=True) + EPS)
    o_ref[...] = (o * r * subln_ref[...] * (1.0 - lambda_init)).astype(o_ref.dtype)


def _attention(qkv, B, S, slopes, dmax, lam, subln_row, lambda_init, tq=512, tk=512):
    N = qkv.shape[0]
    H = ATTN_HEADS
    tq, tk = min(tq, S), min(tk, S)
    assert (tk % tq == 0 or tq % tk == 0) and S % tk == 0 and S % tq == 0 and S <= 128 * 256
    nq = S // tq
    kf, qc = _attn_tables(S)
    kern = functools.partial(_attn_kernel, tq=tq, tk=tk, S=S, lambda_init=lambda_init)
    grid_spec = pltpu.PrefetchScalarGridSpec(
        num_scalar_prefetch=2,
        grid=(B, H, nq),
        in_specs=[
            pl.BlockSpec((tq, LANES), lambda b, h, qi, sl, dm: (b * nq + qi, h)),
            pl.BlockSpec((S, LANES), lambda b, h, qi, sl, dm: (b, H + h)),
            pl.BlockSpec((S, LANES), lambda b, h, qi, sl, dm: (b, 2 * H + h)),
            pl.BlockSpec((S, LANES), lambda b, h, qi, sl, dm: (0, 0)),
            pl.BlockSpec((1, LANES), lambda b, h, qi, sl, dm: (0, 0)),
            pl.BlockSpec(lam.shape, lambda b, h, qi, sl, dm: (0, 0)),
            pl.BlockSpec((1, LANES), lambda b, h, qi, sl, dm: (0, 0)),
        ],
        out_specs=pl.BlockSpec((tq, LANES), lambda b, h, qi, sl, dm: (b * nq + qi, h)),
        scratch_shapes=[
            pltpu.VMEM((2, 2 * tq, 2 * LANES), BF),
            pltpu.VMEM((2 * tq, LANES), F32),
            pltpu.VMEM((2 * tq, LANES), F32),
            pltpu.VMEM((2 * tq, LANES), F32),
        ],
    )
    return pl.pallas_call(
        kern,
        grid_spec=grid_spec,
        out_shape=jax.ShapeDtypeStruct((N, D_MODEL), BF),
        compiler_params=_cp(("arbitrary", "arbitrary", "arbitrary")),
        name="diff_attn",
    )(slopes, dmax, qkv, qkv, qkv, kf, qc, lam, subln_row)


def _log_sigmoid(x):
    return jnp.minimum(x, 0.0) - jnp.log(1.0 + jnp.exp(-jnp.abs(x)))


def _mlstm_kernel(qf_ref, kf_ref, vf_ref, gf_ref, qb_ref, kb_ref, vb_ref, gb_ref, tri_ref, sel_ref,
                  of_ref, ob_ref, state_ref, m_ref):
    c = pl.program_id(1)
    L = MLSTM_CHUNK
    H = MLSTM_HEADS

    @pl.when(c == 0)
    def _():
        state_ref[...] = jnp.zeros(state_ref.shape, F32)
        m_ref[...] = jnp.zeros(m_ref.shape, F32)

    sub = lax.broadcasted_iota(jnp.int32, (2 * MLSTM_DK, L), 0)
    lane = lax.broadcasted_iota(jnp.int32, (L, LANES), 1)
    ones = jnp.ones((L, LANES), BF)

    units, U = [], {}
    for d, (q_ref, k_ref, v_ref, gate_ref, o_ref) in enumerate(
            ((qf_ref, kf_ref, vf_ref, gf_ref, of_ref), (qb_ref, kb_ref, vb_ref, gb_ref, ob_ref))):
        G = gate_ref[...]
        if d == 1:
            G = pltpu.roll(G, LANES - 2 * H, axis=1)
        A = jnp.where((lane >= H) & (lane < 2 * H), _log_sigmoid(G), G)
        AT = A.T
        tri_f = tri_ref[d]
        tri_d = tri_f.astype(BF)
        tri_o = tri_ref[1 - d].astype(BF)
        a1, a2, a3 = _split3(A)
        Bcol = _dot(tri_d, a1) + _dot(tri_d, a2) + _dot(tri_d, a3)
        t1, t2, t3 = _split3(AT)
        Brow = _dot(t1, tri_o) + _dot(t2, tri_o) + _dot(t3, tri_o)
        bsplit = _split3(Bcol)
        mask = tri_f > 0.0
        kT = k_ref[...].astype(F32).T
        q = q_ref[...]
        v = v_ref[...]
        for h in range(H):
            u = (d, h)
            units.append(u)
            own = (sub >= (h % 2) * MLSTM_DK) & (sub < (h % 2 + 1) * MLSTM_DK)
            U[u] = dict(
                d=d, h=h, o_ref=o_ref, mask=mask, bsplit=bsplit,
                ig_row=AT[h:h + 1, :], b_row=Brow[H + h:H + h + 1, :],
                q_pair=q[:, (h // 2) * LANES:(h // 2 + 1) * LANES],
                v_ext=jnp.concatenate([v[:, h * LANES:(h + 1) * LANES], ones], axis=1),
                kT_h=jnp.where(own, kT[(h // 2) * LANES:(h // 2 + 1) * LANES, :], 0.0),
                m=m_ref[d, h], st=state_ref[d, h])

    for u in units:
        x = U[u]
        sel = sel_ref[x["h"]]
        x["b_rep"] = sum(_dot(b, sel) for b in x["bsplit"])
        x["qk"] = _dot(x["q_pair"], x["kT_h"].astype(BF))
        x["qs"] = _dot(x["q_pair"], x["st"].astype(BF))
    for u in units:
        x = U[u]
        b_rep = x["b_rep"]
        x["g"] = b_rep[L - 1:L, :] if x["d"] == 0 else b_rep[0:1, :]
        x["Dm"] = jnp.where(x["mask"], b_rep - (x["b_row"] - x["ig_row"]), -jnp.inf)
        x["inter"] = b_rep + x["m"]
    for u in units:
        x = U[u]
        x["m_t"] = jnp.maximum(x["inter"], jnp.max(x["Dm"], axis=1, keepdims=True))
    for u in units:
        x = U[u]
        w = jnp.exp(x["Dm"] - x["m_t"]) * x["qk"]
        x["wv"] = _dot(w.astype(BF), x["v_ext"])
    for u in units:
        x = U[u]
        h = x["h"]
        s_inter = jnp.exp(x["inter"] - x["m_t"])
        num = s_inter * x["qs"][:, 0:LANES] + x["wv"][:, 0:LANES]
        den = s_inter * x["qs"][:, LANES:2 * LANES] + x["wv"][:, LANES:2 * LANES]
        hout = num / jnp.maximum(jnp.abs(den), jnp.exp(-x["m_t"]))
        x["o_ref"][:, h * LANES:(h + 1) * LANES] = hout.astype(x["o_ref"].dtype)
    for u in units:
        x = U[u]
        a_row = x["g"] - x["b_row"] + x["ig_row"]
        m_new = jnp.maximum(x["g"] + x["m"], jnp.max(a_row, axis=1, keepdims=True))
        wk_row = jnp.exp(a_row - m_new)
        decay = jnp.exp(x["g"] + x["m"] - m_new)
        kw = (x["kT_h"] * wk_row).astype(BF)
        state_ref[x["d"], x["h"]] = (jnp.concatenate([decay, decay], axis=1) * x["st"]
                                     + _dot(kw, x["v_ext"]))
        m_ref[x["d"], x["h"]] = m_new


def _mlstm(proj, gates, B, S):
    N = proj.shape[0]
    L = MLSTM_CHUNK
    nc = S // L
    H = MLSTM_HEADS
    t = np.arange(L)
    lower = (t[None, :] <= t[:, None]).astype(np.float32)
    tri = jnp.asarray(np.stack([lower, lower.T]), dtype=F32)
    sel_np = np.zeros((H, LANES, LANES), np.float32)
    for h in range(H):
        sel_np[h, H + h, :] = 1.0
    sel = jnp.asarray(sel_np, dtype=BF)

    fw = lambda b, c: b * nc + c
    bw = lambda b, c: b * nc + nc - 1 - c

    def chunk_specs(row):
        return [
            pl.BlockSpec((L, H * MLSTM_DK), lambda b, c: (row(b, c), 0)),
            pl.BlockSpec((L, H * MLSTM_DK), lambda b, c: (row(b, c), 1)),
            pl.BlockSpec((L, H * MLSTM_DV), lambda b, c: (row(b, c), 1)),
            pl.BlockSpec((L, LANES), lambda b, c: (row(b, c), 0)),
        ]

    return pl.pallas_call(
        _mlstm_kernel,
        grid=(B, nc),
        in_specs=chunk_specs(fw) + chunk_specs(bw) + [
            pl.BlockSpec((2, L, L), lambda b, c: (0, 0, 0)),
            pl.BlockSpec((H, LANES, LANES), lambda b, c: (0, 0, 0)),
        ],
        out_specs=[
            pl.BlockSpec((L, H * MLSTM_DV), lambda b, c: (fw(b, c), 0)),
            pl.BlockSpec((L, H * MLSTM_DV), lambda b, c: (bw(b, c), 0)),
        ],
        out_shape=[jax.ShapeDtypeStruct((N, H * MLSTM_DV), BF)] * 2,
        scratch_shapes=[
            pltpu.VMEM((2, H, 2 * MLSTM_DK, 2 * LANES), F32),
            pltpu.VMEM((2, H, 1, LANES), F32),
        ],
        compiler_params=_cp(("arbitrary", "arbitrary")),
        name="mlstm",
    )(proj, proj, proj, gates, proj, proj, proj, gates, tri, sel)


def _outproj_tail(a, x_ref, g1_ref, w_ref, n2_ref, sc2_ref, sh2_ref, wr_ref, xo_ref, h2_ref, lg_ref):
    xn = x_ref[...] + g1_ref[...] * _dot(a, w_ref[...])
    xo_ref[...] = xn
    h2 = _norm_mod(xn, n2_ref[...], sc2_ref[...], sh2_ref[...]).astype(BF)
    h2_ref[...] = h2
    lg = _dot_nt(wr_ref[...], h2)
    for cb in range(lg.shape[1] // LANES):
        lg_ref[cb] = lg[:, cb * LANES:(cb + 1) * LANES]


def _outproj_attn_kernel(a_ref, x_ref, g1_ref, w_ref, n2_ref, sc2_ref, sh2_ref, wr_ref,
                         xo_ref, h2_ref, lg_ref):
    _outproj_tail(a_ref[...], x_ref, g1_ref, w_ref, n2_ref, sc2_ref, sh2_ref, wr_ref,
                  xo_ref, h2_ref, lg_ref)


def _outproj_mlstm_kernel(hf_ref, hb_ref, og_ref, on_ref, bd_ref, x_ref, g1_ref, w_ref, n2_ref,
                          sc2_ref, sh2_ref, wr_ref, xo_ref, h2_ref, lg_ref, a_ref):
    for j in range(D_MODEL // 256):
        sl = slice(j * 256, (j + 1) * 256)
        hs = hf_ref[:, sl].astype(F32) + hb_ref[:, sl].astype(F32)
        ss = _group_sumsq(hs, bd_ref[...])
        hn = hs * lax.rsqrt(ss * (1.0 / MLSTM_DV) + EPS) * on_ref[:, sl]
        a_ref[:, sl] = (hn * _sigmoid(og_ref[:, sl].astype(F32))).astype(BF)
    _outproj_tail(a_ref[...], x_ref, g1_ref, w_ref, n2_ref, sc2_ref, sh2_ref, wr_ref,
                  xo_ref, h2_ref, lg_ref)


def _outproj(kind, mix_inputs, x2, S, g1, w_bf, n2, sc2, sh2, wrT_bf, tm=512):
    N = x2.shape[0]
    spb = S // tm
    row = lambda i: (i, 0)
    const = lambda i: (0, 0)
    per_b = lambda i: (i // spb, 0, 0)
    tail_specs = [
        pl.BlockSpec((tm, D_MODEL), row),
        pl.BlockSpec((None, 1, D_MODEL), per_b),
        pl.BlockSpec((D_MODEL, D_MODEL), const),
        pl.BlockSpec((1, D_MODEL), const),
        pl.BlockSpec((None, 1, D_MODEL), per_b),
        pl.BlockSpec((None, 1, D_MODEL), per_b),
        pl.BlockSpec((N_EXPERTS, D_MODEL), const),
    ]
    out_specs = [
        pl.BlockSpec((tm, D_MODEL), row),
        pl.BlockSpec((tm, D_MODEL), row),
        pl.BlockSpec((tm // LANES, N_EXPERTS, LANES), lambda i: (i, 0, 0)),
    ]
    out_shape = [
        jax.ShapeDtypeStruct((N, D_MODEL), F32),
        jax.ShapeDtypeStruct((N, D_MODEL), BF),
        jax.ShapeDtypeStruct((N // LANES, N_EXPERTS, LANES), F32),
    ]
    tail_args = (x2, g1, w_bf, n2, sc2, sh2, wrT_bf)
    if kind == "attn":
        (a,) = mix_inputs
        return pl.pallas_call(
            _outproj_attn_kernel,
            grid=(N // tm,),
            in_specs=[pl.BlockSpec((tm, D_MODEL), row)] + tail_specs,
            out_specs=out_specs, out_shape=out_shape,
            compiler_params=_cp(("arbitrary",)),
            name="outproj_attn",
        )(a, *tail_args)
    h_fw, h_bw, proj, onorm_row = mix_inputs
    bd = _block_diag_ones(MLSTM_DV)
    return pl.pallas_call(
        _outproj_mlstm_kernel,
        grid=(N // tm,),
        in_specs=[
            pl.BlockSpec((tm, D_MODEL), row),
            pl.BlockSpec((tm, D_MODEL), row),
            pl.BlockSpec((tm, D_MODEL), lambda i: (i, 2)),
            pl.BlockSpec((1, D_MODEL), const),
            pl.BlockSpec((256, 256), const),
        ] + tail_specs,
        out_specs=out_specs, out_shape=out_shape,
        scratch_shapes=[pltpu.VMEM((tm, D_MODEL), BF)],
        compiler_params=_cp(("arbitrary",)),
        name="outproj_mlstm",
    )(h_fw, h_bw, proj, onorm_row, bd, *tail_args)


def _router_kernel(lg_ref, triu_ref, rank_ref, gate_ref, offs_ref, bits_ref, *, cap, blocks_per_tb):
    NB = lg_ref.shape[0]
    E = N_EXPERTS
    shape = (E, LANES)

    def softmax_body(b, carry):
        l = lg_ref[b]
        e = jnp.exp(l - jnp.max(l, axis=0, keepdims=True))
        aff = e / jnp.sum(e, axis=0, keepdims=True)
        gate_ref[b] = aff
        bits_ref[b] = pltpu.bitcast(aff, jnp.int32)
        return carry

    lax.fori_loop(0, NB, softmax_body, 0)

    def count(pred):
        def body(b, acc):
            return acc + jnp.where(pred(bits_ref[b]), 1.0, 0.0)
        acc = lax.fori_loop(0, NB, body, jnp.zeros(shape, F32))
        return jnp.broadcast_to(jnp.sum(acc, axis=1, keepdims=True), shape)

    def search_body(i, T):
        cand = T | jnp.left_shift(jnp.int32(1), 30 - i)
        cnt = count(lambda x: x >= cand)
        return jnp.where(cnt >= cap, cand, T)

    T = lax.fori_loop(0, 31, search_body, jnp.zeros(shape, jnp.int32))
    need = cap - count(lambda x: x > T)

    triu = triu_ref[...]
    ones = jnp.ones((LANES, LANES), BF)

    def tb_body(tb, carry):
        ceq, csel = carry
        offs_ref[tb] = csel
        for k in range(blocks_per_tb):
            b = tb * blocks_per_tb + k
            x = bits_ref[b]
            gt = x > T
            eq = x == T
            eqf = jnp.where(eq, 1.0, 0.0)
            eqb = eqf.astype(BF)
            rank_eq = ceq + _dot(eqb, triu) - eqf
            sel = gt | (eq & (rank_eq < need))
            self_ = jnp.where(sel, 1.0, 0.0)
            selb = self_.astype(BF)
            rank = csel + _dot(selb, triu) - self_
            rank_ref[b] = jnp.where(sel, rank, -1.0).astype(jnp.int32)
            gate_ref[b] = jnp.where(sel, gate_ref[b], 0.0)
            ceq = ceq + _dot(eqb, ones)
            csel = csel + _dot(selb, ones)
        return ceq, csel

    n_tb = NB // blocks_per_tb
    _, csel = lax.fori_loop(0, n_tb, tb_body, (jnp.zeros(shape, F32), jnp.zeros(shape, F32)))
    offs_ref[n_tb] = csel


def _router(lg, cap, tb_tokens):
    NB = lg.shape[0]
    bpt = tb_tokens // LANES
    n_tb = NB // bpt
    u = np.arange(LANES)
    triu = jnp.asarray((u[:, None] <= u[None, :]).astype(np.float32), dtype=BF)
    kern = functools.partial(_router_kernel, cap=float(cap), blocks_per_tb=bpt)
    return pl.pallas_call(
        kern,
        out_shape=[
            jax.ShapeDtypeStruct((NB, N_EXPERTS, LANES), jnp.int32),
            jax.ShapeDtypeStruct((NB, N_EXPERTS, LANES), F32),
            jax.ShapeDtypeStruct((n_tb + 1, N_EXPERTS, LANES), F32),
        ],
        scratch_shapes=[pltpu.VMEM((NB, N_EXPERTS, LANES), jnp.int32)],
        compiler_params=pltpu.CompilerParams(vmem_limit_bytes=VMEM_LIMIT),
        name="router",
    )(lg, triu)


def _schedule(offs, cap, slot_tile, token_major):
    E = N_EXPERTS
    n_t = offs.shape[0] - 1
    n_s = cap // slot_tile
    lo = offs[:-1]
    hi = offs[1:]
    nonempty = hi > lo
    s_lo = jnp.minimum(lo // slot_tile, n_s - 1)
    s_hi = jnp.where(nonempty, (hi - 1) // slot_tile, s_lo)
    cnt = jnp.where(nonempty, s_hi - s_lo + 1, 0)
    if token_major:
        cnt = cnt.at[:, 0].set(jnp.maximum(cnt[:, 0], 1))
        cnt_flat, slo_flat = cnt.reshape(-1), s_lo.reshape(-1)
        n_steps = E * (n_s + n_t) + n_t
    else:
        cnt_flat, slo_flat = cnt.T.reshape(-1), s_lo.T.reshape(-1)
        n_steps = E * (n_s + n_t)
    ends = jnp.cumsum(cnt_flat)
    total = ends[-1]
    i = jnp.arange(n_steps, dtype=jnp.int32)
    ic = jnp.minimum(i, total - 1)
    before = ends[None, :-1] <= ic[:, None]
    pair = jnp.sum(before, axis=1).astype(jnp.int32)
    val = slo_flat - (ends - cnt_flat)
    sv = val[0] + jnp.sum(jnp.where(before, (val[1:] - val[:-1])[None, :], 0), axis=1)
    s = (ic + sv).astype(jnp.int32)
    if token_major:
        t, e = pair // E, pair % E
        key = t
    else:
        e, t = pair // n_t, pair % n_t
        key = e * n_s + s
    valid = i < total
    change = key[1:] != key[:-1]
    first = valid & jnp.concatenate([jnp.ones((1,), bool), change])
    last = valid & jnp.concatenate([change | ~valid[1:], jnp.ones((1,), bool)])
    flags = first.astype(jnp.int32) + 2 * last.astype(jnp.int32) + 4 * valid.astype(jnp.int32)
    return e.astype(jnp.int32), s, t.astype(jnp.int32), flags


def _one_hot_slots(rank_ref, e, slot0, T, blocks):
    slot = slot0 + lax.broadcasted_iota(jnp.int32, (T, LANES), 0)
    pieces = [jnp.where(rank_ref[a, pl.ds(e, 1), :] == slot, 1.0, 0.0).astype(BF) for a in blocks]
    return jnp.concatenate(pieces, axis=1)


def _ffn_kernel(se_ref, ss_ref, st_ref, fl_ref, offs_ref, x_ref, rank_ref, gate_ref, wg_ref, wu_ref,
                wd_ref, ye_ref, xacc_ref, gacc_ref, *, T, sub, n_sub, f_chunk):
    i = pl.program_id(0)
    e = se_ref[i]
    s = ss_ref[i]
    fl = fl_ref[i]
    bps = sub // LANES

    @pl.when((fl & 1) != 0)
    def _():
        xacc_ref[...] = jnp.zeros(xacc_ref.shape, F32)
        gacc_ref[...] = jnp.zeros(gacc_ref.shape, F32)

    for a in range(n_sub):
        tsub = st_ref[i] * n_sub + a
        lo = offs_ref[tsub * N_EXPERTS + e]
        hi = offs_ref[(tsub + 1) * N_EXPERTS + e]

        @pl.when(((fl & 4) != 0) & (hi > s * T) & (lo < (s + 1) * T) & (hi > lo))
        def _(a=a):
            blocks = range(a * bps, (a + 1) * bps)
            P = _one_hot_slots(rank_ref, e, s * T, T, blocks)
            xacc_ref[...] += _dot(P, x_ref[a * sub:(a + 1) * sub, :])
            g = jnp.concatenate([gate_ref[b, pl.ds(e, 1), :] for b in blocks], axis=1)
            parts = [t.astype(F32) for t in _split3(g)]
            grows = jnp.concatenate(parts + [jnp.zeros((13, sub), F32)], axis=0).astype(BF)
            gacc_ref[...] += _dot_nt(P, grows)

    @pl.when((fl & 2) != 0)
    def _():
        x = xacc_ref[...].astype(BF)
        y = jnp.zeros((T, D_MODEL), F32)
        for c in range(EXPERT_FF // f_chunk):
            sl = slice(c * f_chunk, (c + 1) * f_chunk)
            gt = _dot(x, wg_ref[:, sl])
            up = _dot(x, wu_ref[:, sl])
            hid = (gt * _sigmoid(gt) * up).astype(BF)
            y = y + _dot(hid, wd_ref[sl, :])
        ga = gacc_ref[...]
        gcol = ga[:, 0:1] + ga[:, 1:2] + ga[:, 2:3]
        ye_ref[...] = (y * gcol).astype(ye_ref.dtype)


def _ffn(sched, offs_flat, h2, rankm, gate, wg, wu, wd, cap, T, sub, n_sub):
    se, ss, st, fl = sched
    n_steps = se.shape[0]
    TB = sub * n_sub
    nb = TB // LANES
    kern = functools.partial(_ffn_kernel, T=T, sub=sub, n_sub=n_sub, f_chunk=512)
    grid_spec = pltpu.PrefetchScalarGridSpec(
        num_scalar_prefetch=5,
        grid=(n_steps,),
        in_specs=[
            pl.BlockSpec((TB, D_MODEL), lambda i, se, ss, st, fl, of: (st[i], 0)),
            pl.BlockSpec((nb, N_EXPERTS, LANES), lambda i, se, ss, st, fl, of: (st[i], 0, 0)),
            pl.BlockSpec((nb, N_EXPERTS, LANES), lambda i, se, ss, st, fl, of: (st[i], 0, 0)),
            pl.BlockSpec((None, D_MODEL, EXPERT_FF), lambda i, se, ss, st, fl, of: (se[i], 0, 0)),
            pl.BlockSpec((None, D_MODEL, EXPERT_FF), lambda i, se, ss, st, fl, of: (se[i], 0, 0)),
            pl.BlockSpec((None, EXPERT_FF, D_MODEL), lambda i, se, ss, st, fl, of: (se[i], 0, 0)),
        ],
        out_specs=pl.BlockSpec((None, T, D_MODEL), lambda i, se, ss, st, fl, of: (se[i], ss[i], 0)),
        scratch_shapes=[pltpu.VMEM((T, D_MODEL), F32), pltpu.VMEM((T, 16), F32)],
    )
    return pl.pallas_call(
        kern,
        grid_spec=grid_spec,
        out_shape=jax.ShapeDtypeStruct((N_EXPERTS, cap, D_MODEL), BF),
        compiler_params=_cp(("arbitrary",)),
        name="moe_ffn",
    )(se, ss, st, fl, offs_flat, h2, rankm, gate, wg, wu, wd)


def _combine_kernel(se_ref, ss_ref, st_ref, fl_ref, offs_ref, ye_ref, rank_ref, x_ref, g2_ref, o_ref,
                    acc_ref, *, T, n_tiles, n_blocks, subs_per_tile):
    i = pl.program_id(0)
    e = se_ref[i]
    s = ss_ref[i]
    t = st_ref[i]
    fl = fl_ref[i]

    @pl.when((fl & 1) != 0)
    def _():
        acc_ref[...] = jnp.zeros(acc_ref.shape, F32)

    lo = offs_ref[t * subs_per_tile * N_EXPERTS + e]
    hi = offs_ref[(t + 1) * subs_per_tile * N_EXPERTS + e]
    for j in range(n_tiles):
        slot0 = (s * n_tiles + j) * T

        @pl.when(((fl & 4) != 0) & (hi > slot0) & (lo < slot0 + T) & (hi > lo))
        def _(j=j, slot0=slot0):
            P = _one_hot_slots(rank_ref, e, slot0, T, range(n_blocks))
            acc_ref[...] += _dot_tn(P, ye_ref[j * T:(j + 1) * T, :])

    @pl.when((fl & 2) != 0)
    def _():
        o_ref[...] = x_ref[...] + g2_ref[...] * acc_ref[...]


def _combine(sched, offs_flat, ye, rankm, x1, g2, S, T, n_tiles, TT, sub):
    se, ss, st, fl = sched
    n_steps = se.shape[0]
    N = x1.shape[0]
    nb = TT // LANES
    spb = S // TT
    YB = T * n_tiles
    kern = functools.partial(_combine_kernel, T=T, n_tiles=n_tiles, n_blocks=nb, subs_per_tile=TT // sub)
    grid_spec = pltpu.PrefetchScalarGridSpec(
        num_scalar_prefetch=5,
        grid=(n_steps,),
        in_specs=[
            pl.BlockSpec((None, YB, D_MODEL), lambda i, se, ss, st, fl, of: (se[i], ss[i], 0)),
            pl.BlockSpec((nb, N_EXPERTS, LANES), lambda i, se, ss, st, fl, of: (st[i], 0, 0)),
            pl.BlockSpec((TT, D_MODEL), lambda i, se, ss, st, fl, of: (st[i], 0)),
            pl.BlockSpec((None, 1, D_MODEL), lambda i, se, ss, st, fl, of: (st[i] // spb, 0, 0)),
        ],
        out_specs=pl.BlockSpec((TT, D_MODEL), lambda i, se, ss, st, fl, of: (st[i], 0)),
        scratch_shapes=[pltpu.VMEM((TT, D_MODEL), F32)],
    )
    return pl.pallas_call(
        kern,
        grid_spec=grid_spec,
        out_shape=jax.ShapeDtypeStruct((N, D_MODEL), F32),
        compiler_params=_cp(("arbitrary",)),
        name="moe_combine",
    )(se, ss, st, fl, offs_flat, ye, rankm, x1, g2)


MOE_T = 256
MOE_SUB = 512
FFN_SUBS = 4
COMBINE_TT = 1024
YE_TILES = 2


def _moe(h2, lg, x1, g2, S, wg, wu, wd):
    N = x1.shape[0]
    cap = (CAPACITY_FACTOR * N) // N_EXPERTS
    T = min(MOE_T, cap)
    sub = min(MOE_SUB, S)
    n_sub = min(FFN_SUBS, S // sub)
    TT = min(COMBINE_TT, S)
    n_tiles = min(YE_TILES, cap // T)
    rankm, gate, offs = _router(lg, cap, sub)
    offs_i = offs[:, :, 0].astype(jnp.int32)
    offs_flat = offs_i.reshape(-1)
    fsched = _schedule(offs_i[::n_sub], cap, T, token_major=False)
    csched = _schedule(offs_i[::TT // sub], cap, T * n_tiles, token_major=True)
    ye = _ffn(fsched, offs_flat, h2, rankm, gate, wg, wu, wd, cap, T, sub, n_sub)
    return _combine(csched, offs_flat, ye, rankm, x1, g2, S, T, n_tiles, TT, sub)


def _trunk(x, mods, P):
    B, S, _ = x.shape
    x2 = x.reshape(B * S, D_MODEL)
    for l in range(DEPTH):
        mod = mods[l]
        sh1, sc1, g1, sh2, sc2, g2 = [mod[:, j][:, None, :] for j in range(N_MOD)]
        n1 = P["norm_g"][l, 0][None, :]
        n2 = P["norm_g"][l, 1][None, :]
        j = l // 2
        if l % 2 == 0:
            lambda_init = 0.8 - 0.6 * math.exp(-0.3 * l)
            qkv = _inproj_attn(x2, S, n1, sc1, sh1, P["attn_w_in"][j], P["attn_gain"][j])
            a = _attention(qkv, B, S, P["slopes"], P["attn_dmax"][j], P["attn_lambda"][j],
                           P["attn_subln"][j][None, :], lambda_init, tq=1024 if S >= 8192 else 512)
            x1, h2, lg = _outproj("attn", (a,), x2, S, g1, P["attn_w_out"][j], n2, sc2, sh2,
                                  P["w_routerT"][l])
        else:
            proj, gates = _inproj_mlstm(x2, S, n1, sc1, sh1, P["mlstm_w_main"][j], P["mlstm_w_gate"][j],
                                        P["mlstm_b_gate"][j])
            h_fw, h_bw = _mlstm(proj, gates, B, S)
            x1, h2, lg = _outproj("mlstm", (h_fw, h_bw, proj, P["mlstm_out_norm"][j]), x2, S, g1,
                                  P["mlstm_w_out"][j], n2, sc2, sh2, P["w_routerT"][l])
        x2 = _moe(h2, lg, x1, g2, S, P["w_exp_gate"][l], P["w_exp_up"][l], P["w_exp_down"][l])
    return x2.reshape(B, S, D_MODEL)


def _prepare(norm_g, attn_w_in, attn_q_gain, attn_k_gain, attn_lambda, attn_subln, attn_w_out,
             mlstm_w_in, mlstm_b_gate, mlstm_out_norm, mlstm_w_out, w_router, w_exp_gate, w_exp_up,
             w_exp_down):
    H = ATTN_HEADS
    n_main = 2 * MLSTM_HEADS * MLSTM_DK + MLSTM_HEADS * MLSTM_DV + D_MODEL
    n_gate = 4 * MLSTM_HEADS
    qg = jnp.tile(attn_q_gain * (ATTN_HEAD_DIM ** -0.5 * _LOG2E), (1, 2 * H))
    kg = jnp.tile(attn_k_gain, (1, 2 * H))
    slopes = 2.0 ** (-8.0 * np.arange(1, H + 1) / H)
    assert all(np.frexp(slopes)[0] == 0.5), "ALiBi slopes must be powers of two for bf16-exact features"
    smax = (8.0 * _LOG2E * 1.02) * jnp.max(jnp.abs(attn_q_gain), axis=1) * jnp.max(jnp.abs(attn_k_gain), axis=1)
    dmax = jnp.ceil((152.0 + 2.0 * smax[:, None]) / jnp.asarray(slopes * _LOG2E, F32)[None, :])
    dmax = jnp.clip(dmax, 0.0, 2.0 ** 30).astype(jnp.int32)
    return {
        "attn_dmax": dmax,
        "norm_g": norm_g,
        "attn_w_in": attn_w_in.astype(BF),
        "attn_gain": jnp.concatenate([qg, kg], axis=1)[:, None, :],
        "attn_lambda": attn_lambda,
        "attn_subln": attn_subln,
        "attn_w_out": attn_w_out.astype(BF),
        "slopes": jnp.asarray(slopes, F32),
        "mlstm_w_main": mlstm_w_in[:, :, :n_main].astype(BF),
        "mlstm_w_gate": jnp.pad(mlstm_w_in[:, :, n_main:], ((0, 0), (0, 0), (0, LANES - n_gate))).astype(BF),
        "mlstm_b_gate": jnp.pad(mlstm_b_gate, ((0, 0), (0, LANES - n_gate)))[:, None, :],
        "mlstm_out_norm": jnp.tile(mlstm_out_norm, (1, MLSTM_HEADS))[:, None, :],
        "mlstm_w_out": mlstm_w_out.astype(BF),
        "w_routerT": jnp.swapaxes(w_router, 1, 2).astype(BF),
        "w_exp_gate": w_exp_gate.astype(BF),
        "w_exp_up": w_exp_up.astype(BF),
        "w_exp_down": w_exp_down.astype(BF),
    }


def kernel(x_prompt, x_sample, c_prompt, c_sample, norm_g, w_ada, b_ada, attn_w_in, attn_q_gain,
           attn_k_gain, attn_lambda, attn_subln, attn_w_out, mlstm_w_in, mlstm_b_gate, mlstm_out_norm,
           mlstm_w_out, w_router, w_exp_gate, w_exp_up, w_exp_down):
    P = _prepare(norm_g, attn_w_in, attn_q_gain, attn_k_gain, attn_lambda, attn_subln, attn_w_out,
                 mlstm_w_in, mlstm_b_gate, mlstm_out_norm, mlstm_w_out, w_router, w_exp_gate,
                 w_exp_up, w_exp_down)
    bp, bs = c_prompt.shape[0], c_sample.shape[0]
    pad = (-(bp + bs)) % 8
    c_all = jnp.concatenate([c_prompt, c_sample, jnp.zeros((pad, D_MODEL), F32)], axis=0)
    mods = _ada(c_all, w_ada, b_ada)
    mods = mods.reshape(DEPTH, c_all.shape[0], N_MOD, D_MODEL)
    y_prompt = _trunk(x_prompt, mods[:, :bp], P)
    y_sample = _trunk(x_sample, mods[:, bp:bp + bs], P)
    return (y_prompt, y_sample)
```

```python
import functools
import math

import jax
import jax.numpy as jnp
import numpy as np
from jax import lax
from jax.experimental import pallas as pl
from jax.experimental.pallas import tpu as pltpu

D_MODEL = 1024
DEPTH = 4
ATTN_HEADS = 8
ATTN_HEAD_DIM = 64
MLSTM_HEADS = 8
MLSTM_DK = 64
MLSTM_DV = 128
MLSTM_CHUNK = 128
N_EXPERTS = 16
CAPACITY_FACTOR = 2
EXPERT_FF = 2048
N_MOD = 6
EPS = 1e-6

LANES = 128
VMEM_LIMIT = 56 * 1024 * 1024

BF = jnp.bfloat16
F32 = jnp.float32
NEG = -1e30


def _cp(sem, vmem=VMEM_LIMIT):
    return pltpu.CompilerParams(dimension_semantics=sem, vmem_limit_bytes=vmem)


def _sigmoid(x):
    return 1.0 / (1.0 + jnp.exp(-x))


def _dot(a, b):
    return jnp.dot(a, b, preferred_element_type=F32)


def _dot_nt(a, b):
    return lax.dot_general(a, b, (((1,), (1,)), ((), ())), preferred_element_type=F32)


def _dot_tn(a, b):
    return lax.dot_general(a, b, (((0,), (0,)), ((), ())), preferred_element_type=F32)


def _split3(x):
    a = x.astype(BF)
    r = x - a.astype(F32)
    b = r.astype(BF)
    c = (r - b.astype(F32)).astype(BF)
    return a, b, c


def _ada_kernel(c_ref, w_ref, b_ref, o_ref):
    c = c_ref[...]
    s = (c * _sigmoid(c)).astype(BF)
    o_ref[...] = _dot(s, w_ref[...].astype(BF)) + b_ref[...]


def _ada(c_all, w_ada, b_ada):
    bp = c_all.shape[0]
    n_out = N_MOD * D_MODEL
    tn = 1536
    return pl.pallas_call(
        _ada_kernel,
        grid=(DEPTH, n_out // tn),
        in_specs=[
            pl.BlockSpec((bp, D_MODEL), lambda l, j: (0, 0)),
            pl.BlockSpec((None, D_MODEL, tn), lambda l, j: (l, 0, j)),
            pl.BlockSpec((None, 1, tn), lambda l, j: (l, 0, j)),
        ],
        out_specs=pl.BlockSpec((None, bp, tn), lambda l, j: (l, 0, j)),
        out_shape=jax.ShapeDtypeStruct((DEPTH, bp, n_out), F32),
        compiler_params=_cp(("arbitrary", "arbitrary")),
        name="ada",
    )(c_all, w_ada, b_ada.reshape(DEPTH, 1, n_out))


def _norm_mod(x, g, sc, sh):
    r = lax.rsqrt(jnp.mean(x * x, axis=-1, keepdims=True) + EPS)
    return x * r * g * (1.0 + sc) + sh


def _group_sumsq(y, bd):
    return _dot((y * y).astype(BF), bd)


def _block_diag_ones(group, size=256):
    assert group & (group - 1) == 0
    i = np.arange(size) // group
    return jnp.asarray((i[:, None] == i[None, :]).astype(np.float32) / group, dtype=BF)


def _inproj_attn_kernel(x_ref, g_ref, sc_ref, sh_ref, w_ref, gain_ref, bd_ref, o_ref):
    h = _norm_mod(x_ref[...], g_ref[...], sc_ref[...], sh_ref[...]).astype(BF)
    wide = 512
    n_norm = gain_ref.shape[1] // wide
    for j in range(o_ref.shape[1] // wide):
        yw = _dot(h, w_ref[:, j * wide:(j + 1) * wide])
        for c in range(wide // 256):
            sl = slice(j * wide + c * 256, j * wide + (c + 1) * 256)
            y = yw[:, c * 256:(c + 1) * 256]
            if j < n_norm:
                ms = _group_sumsq(y, bd_ref[...])
                y = y * (lax.rsqrt(ms + EPS) * gain_ref[:, sl])
            o_ref[:, sl] = y.astype(o_ref.dtype)


def _inproj_attn(x2, S, g, sc, sh, w_bf, gain_row, tm=512):
    N = x2.shape[0]
    n_out = w_bf.shape[1]
    spb = S // tm
    bd = _block_diag_ones(ATTN_HEAD_DIM)
    return pl.pallas_call(
        _inproj_attn_kernel,
        grid=(N // tm,),
        in_specs=[
            pl.BlockSpec((tm, D_MODEL), lambda i: (i, 0)),
            pl.BlockSpec((1, D_MODEL), lambda i: (0, 0)),
            pl.BlockSpec((None, 1, D_MODEL), lambda i: (i // spb, 0, 0)),
            pl.BlockSpec((None, 1, D_MODEL), lambda i: (i // spb, 0, 0)),
            pl.BlockSpec((D_MODEL, n_out), lambda i: (0, 0)),
            pl.BlockSpec((1, gain_row.shape[1]), lambda i: (0, 0)),
            pl.BlockSpec((256, 256), lambda i: (0, 0)),
        ],
        out_specs=pl.BlockSpec((tm, n_out), lambda i: (i, 0)),
        out_shape=jax.ShapeDtypeStruct((N, n_out), BF),
        compiler_params=_cp(("arbitrary",)),
        name="inproj_attn",
    )(x2, g, sc, sh, w_bf, gain_row, bd)


def _inproj_mlstm_kernel(x_ref, g_ref, sc_ref, sh_ref, w_ref, wg_ref, bg_ref, o_ref, og_ref):
    h = _norm_mod(x_ref[...], g_ref[...], sc_ref[...], sh_ref[...]).astype(BF)
    hk = MLSTM_HEADS * MLSTM_DK
    for j in range(o_ref.shape[1] // 256):
        sl = slice(j * 256, (j + 1) * 256)
        y = _dot(h, w_ref[:, sl])
        if hk <= j * 256 < 2 * hk:
            y = y * (MLSTM_DK ** -0.5)
        o_ref[:, sl] = y.astype(o_ref.dtype)
    og_ref[...] = _dot(h, wg_ref[...]) + bg_ref[...]


def _inproj_mlstm(x2, S, g, sc, sh, w_bf, wg_bf, bg_row, tm=512):
    N = x2.shape[0]
    n_out = w_bf.shape[1]
    spb = S // tm
    return pl.pallas_call(
        _inproj_mlstm_kernel,
        grid=(N // tm,),
        in_specs=[
            pl.BlockSpec((tm, D_MODEL), lambda i: (i, 0)),
            pl.BlockSpec((1, D_MODEL), lambda i: (0, 0)),
            pl.BlockSpec((None, 1, D_MODEL), lambda i: (i // spb, 0, 0)),
            pl.BlockSpec((None, 1, D_MODEL), lambda i: (i // spb, 0, 0)),
            pl.BlockSpec((D_MODEL, n_out), lambda i: (0, 0)),
            pl.BlockSpec((D_MODEL, LANES), lambda i: (0, 0)),
            pl.BlockSpec((1, LANES), lambda i: (0, 0)),
        ],
        out_specs=[
            pl.BlockSpec((tm, n_out), lambda i: (i, 0)),
            pl.BlockSpec((tm, LANES), lambda i: (i, 0)),
        ],
        out_shape=[
            jax.ShapeDtypeStruct((N, n_out), BF),
            jax.ShapeDtypeStruct((N, LANES), F32),
        ],
        compiler_params=_cp(("arbitrary",)),
        name="inproj_mlstm",
    )(x2, g, sc, sh, w_bf, wg_bf, bg_row)


_LOG2E = math.log2(math.e)
N_FEAT = 12


def _log2e_split():
    c = np.float32(_LOG2E)
    out = []
    for _ in range(3):
        t = np.asarray(c, dtype=BF).astype(np.float32)
        out.append(float(t))
        c = np.float32(c - t)
    return out


def _attn_tables(S):
    c = _log2e_split()
    pos = np.arange(S)
    kf = np.zeros((S, LANES), np.float32)
    qc = np.zeros((1, LANES), np.float32)
    for t in range(3):
        kf[:, t] = -128.0 * c[t]
        kf[:, 3 + t] = -c[t]
        kf[:, 6 + t] = pos // 128
        kf[:, 9 + t] = pos % 128
        qc[0, 6 + t] = 128.0 * c[t]
        qc[0, 9 + t] = c[t]
    return jnp.asarray(kf, dtype=BF), jnp.asarray(qc, dtype=F32)


def _attn_kernel(slopes_ref, dmax_ref, q_ref, k_ref, v_ref, kf_ref, qc_ref, lam_ref, subln_ref, o_ref,
                 qs_ref, m_ref, l_ref, acc_ref, *, tq, tk, S, lambda_init):
    h = pl.program_id(1)
    qi = pl.program_id(2)
    nk = S // tk
    slope = slopes_ref[h]
    r0 = qi * tq

    q = q_ref[...]
    lane = lax.broadcasted_iota(jnp.int32, (tq, LANES), 1)
    zero = jnp.zeros_like(q)
    q0 = jnp.where(lane < ATTN_HEAD_DIM, q, zero)
    q1 = jnp.where(lane >= ATTN_HEAD_DIM, q, zero)
    pos = r0 + lax.broadcasted_iota(jnp.int32, (tq, LANES), 0)
    hi = (pos >> 7).astype(F32)
    lo = (pos & 127).astype(F32)
    feat = jnp.where(lane < 3, hi, jnp.where(lane < 6, lo, qc_ref[...])) * slope
    for sgn, f in ((0, feat.astype(BF)), (1, (-feat).astype(BF))):
        qs_ref[sgn, 0:tq, 0:LANES] = q0
        qs_ref[sgn, tq:2 * tq, 0:LANES] = q1
        qs_ref[sgn, 0:tq, LANES:2 * LANES] = f
        qs_ref[sgn, tq:2 * tq, LANES:2 * LANES] = f

    def scores(ki, sgn):
        off = pl.multiple_of(ki * tk, tk)
        kaug = jnp.concatenate([k_ref[pl.ds(off, tk), :], kf_ref[pl.ds(off, tk), :]], axis=1)
        return _dot_nt(qs_ref[sgn], kaug), off

    def diag_correction(off):
        row = lax.broadcasted_iota(jnp.int32, (tq, tk), 0)
        col = lax.broadcasted_iota(jnp.int32, (tq, tk), 1)
        corr = jnp.minimum(row - col + (r0 - off), 0).astype(F32) * (2.0 * _LOG2E * slope)
        return jnp.concatenate([corr, corr], axis=0)

    def chunk(ki, sgn, state, diagonal=False):
        m_prev, l_prev, acc_prev = state
        s, off = scores(ki, sgn)
        if diagonal:
            s = s + diag_correction(off)
        m_new = jnp.maximum(m_prev, jnp.max(s, axis=1, keepdims=True))
        p = jnp.exp2(s - jnp.concatenate([m_new] * (tk // LANES), axis=1))
        alpha = jnp.exp2(m_prev - m_new)
        l_new = alpha * l_prev + jnp.sum(p, axis=1, keepdims=True)
        acc_new = alpha * acc_prev + _dot(p.astype(BF), v_ref[pl.ds(off, tk), :])
        return m_new, l_new, acc_new

    def load_state():
        return m_ref[...], l_ref[...], acc_ref[...]

    def store_state(state):
        m_ref[...], l_ref[...], acc_ref[...] = state

    n_diag = max(1, tq // tk)
    kd = r0 // tk
    s, off = scores(kd, 0)
    s = s + diag_correction(off)
    m0 = jnp.max(s, axis=1, keepdims=True)
    p = jnp.exp2(s - m0)
    state = (jnp.broadcast_to(m0, m_ref.shape),
             jnp.broadcast_to(jnp.sum(p, axis=1, keepdims=True), l_ref.shape),
             _dot(p.astype(BF), v_ref[pl.ds(off, tk), :]))
    for j in range(1, n_diag):
        state = chunk(kd + j, 0, state, diagonal=True)
    store_state(state)

    dmax = dmax_ref[h]
    k_lo = jnp.maximum(r0 - dmax, 0) // tk
    k_hi = jnp.minimum((r0 + tq - 1 + dmax) // tk, nk - 1)
    n_before = kd - k_lo
    n_off = n_before + (k_hi - (kd + n_diag - 1))

    def nth(i):
        after = (i >= n_before).astype(jnp.int32)
        return k_lo + i + after * n_diag, after

    def pair_body(j, c):
        state = load_state()
        state = chunk(*nth(2 * j), state)
        state = chunk(*nth(2 * j + 1), state)
        store_state(state)
        return c

    lax.fori_loop(0, n_off // 2, pair_body, 0)

    @pl.when(n_off % 2 == 1)
    def _():
        store_state(chunk(*nth(n_off - 1), load_state()))


    lam = lam_ref[...]
    l01 = jnp.sum(lam[0:1, :] * lam[1:2, :], axis=1, keepdims=True)
    l23 = jnp.sum(lam[2:3, :] * lam[3:4, :], axis=1, keepdims=True)
    lmbda = jnp.exp(l01) - jnp.exp(l23) + lambda_init
    o0 = acc_ref[0:tq, :] / l_ref[0:tq, :]
    o1 = acc_ref[tq:2 * tq, :] / l_ref[tq:2 * tq, :]
    o = o0 - lmbda * o1
    r = lax.rsqrt(jnp.mean(o * o, axis=-1, keepdims=True) + EPS)
    o_ref[...] = (o * r * subln_ref[...] * (1.0 - lambda_init)).astype(o_ref.dtype)


def _attention(qkv, B, S, slopes, dmax, lam, subln_row, lambda_init, tq=512, tk=512):
    N = qkv.shape[0]
    H = ATTN_HEADS
    tq, tk = min(tq, S), min(tk, S)
    assert (tk % tq == 0 or tq % tk == 0) and S % tk == 0 and S % tq == 0 and S <= 128 * 256
    nq = S // tq
    kf, qc = _attn_tables(S)
    kern = functools.partial(_attn_kernel, tq=tq, tk=tk, S=S, lambda_init=lambda_init)
    grid_spec = pltpu.PrefetchScalarGridSpec(
        num_scalar_prefetch=2,
        grid=(B, H, nq),
        in_specs=[
            pl.BlockSpec((tq, LANES), lambda b, h, qi, sl, dm: (b * nq + qi, h)),
            pl.BlockSpec((S, LANES), lambda b, h, qi, sl, dm: (b, H + h)),
            pl.BlockSpec((S, LANES), lambda b, h, qi, sl, dm: (b, 2 * H + h)),
            pl.BlockSpec((S, LANES), lambda b, h, qi, sl, dm: (0, 0)),
            pl.BlockSpec((1, LANES), lambda b, h, qi, sl, dm: (0, 0)),
            pl.BlockSpec(lam.shape, lambda b, h, qi, sl, dm: (0, 0)),
            pl.BlockSpec((1, LANES), lambda b, h, qi, sl, dm: (0, 0)),
        ],
        out_specs=pl.BlockSpec((tq, LANES), lambda b, h, qi, sl, dm: (b * nq + qi, h)),
        scratch_shapes=[
            pltpu.VMEM((2, 2 * tq, 2 * LANES), BF),
            pltpu.VMEM((2 * tq, LANES), F32),
            pltpu.VMEM((2 * tq, LANES), F32),
            pltpu.VMEM((2 * tq, LANES), F32),
        ],
    )
    return pl.pallas_call(
        kern,
        grid_spec=grid_spec,
        out_shape=jax.ShapeDtypeStruct((N, D_MODEL), BF),
        compiler_params=_cp(("arbitrary", "arbitrary", "arbitrary")),
        name="diff_attn",
    )(slopes, dmax, qkv, qkv, qkv, kf, qc, lam, subln_row)


def _log_sigmoid(x):
    return jnp.minimum(x, 0.0) - jnp.log(1.0 + jnp.exp(-jnp.abs(x)))


def _mlstm_kernel(qf_ref, kf_ref, vf_ref, gf_ref, qb_ref, kb_ref, vb_ref, gb_ref, tri_ref, sel_ref,
                  of_ref, ob_ref, state_ref, m_ref):
    c = pl.program_id(1)
    L = MLSTM_CHUNK
    H = MLSTM_HEADS

    @pl.when(c == 0)
    def _():
        state_ref[...] = jnp.zeros(state_ref.shape, F32)
        m_ref[...] = jnp.zeros(m_ref.shape, F32)

    sub = lax.broadcasted_iota(jnp.int32, (2 * MLSTM_DK, L), 0)
    lane = lax.broadcasted_iota(jnp.int32, (L, LANES), 1)
    ones = jnp.ones((L, LANES), BF)

    units, U = [], {}
    for d, (q_ref, k_ref, v_ref, gate_ref, o_ref) in enumerate(
            ((qf_ref, kf_ref, vf_ref, gf_ref, of_ref), (qb_ref, kb_ref, vb_ref, gb_ref, ob_ref))):
        G = gate_ref[...]
        if d == 1:
            G = pltpu.roll(G, LANES - 2 * H, axis=1)
        A = jnp.where((lane >= H) & (lane < 2 * H), _log_sigmoid(G), G)
        AT = A.T
        tri_f = tri_ref[d]
        tri_d = tri_f.astype(BF)
        tri_o = tri_ref[1 - d].astype(BF)
        a1, a2, a3 = _split3(A)
        Bcol = _dot(tri_d, a1) + _dot(tri_d, a2) + _dot(tri_d, a3)
        t1, t2, t3 = _split3(AT)
        Brow = _dot(t1, tri_o) + _dot(t2, tri_o) + _dot(t3, tri_o)
        bsplit = _split3(Bcol)
        mask = tri_f > 0.0
        kT = k_ref[...].astype(F32).T
        q = q_ref[...]
        v = v_ref[...]
        for h in range(H):
            u = (d, h)
            units.append(u)
            own = (sub >= (h % 2) * MLSTM_DK) & (sub < (h % 2 + 1) * MLSTM_DK)
            U[u] = dict(
                d=d, h=h, o_ref=o_ref, mask=mask, bsplit=bsplit,
                ig_row=AT[h:h + 1, :], b_row=Brow[H + h:H + h + 1, :],
                q_pair=q[:, (h // 2) * LANES:(h // 2 + 1) * LANES],
                v_ext=jnp.concatenate([v[:, h * LANES:(h + 1) * LANES], ones], axis=1),
                kT_h=jnp.where(own, kT[(h // 2) * LANES:(h // 2 + 1) * LANES, :], 0.0),
                m=m_ref[d, h], st=state_ref[d, h])

    for u in units:
        x = U[u]
        sel = sel_ref[x["h"]]
        x["b_rep"] = sum(_dot(b, sel) for b in x["bsplit"])
        x["qk"] = _dot(x["q_pair"], x["kT_h"].astype(BF))
        x["qs"] = _dot(x["q_pair"], x["st"].astype(BF))
    for u in units:
        x = U[u]
        b_rep = x["b_rep"]
        x["g"] = b_rep[L - 1:L, :] if x["d"] == 0 else b_rep[0:1, :]
        x["Dm"] = jnp.where(x["mask"], b_rep - (x["b_row"] - x["ig_row"]), -jnp.inf)
        x["inter"] = b_rep + x["m"]
    for u in units:
        x = U[u]
        x["m_t"] = jnp.maximum(x["inter"], jnp.max(x["Dm"], axis=1, keepdims=True))
    for u in units:
        x = U[u]
        w = jnp.exp(x["Dm"] - x["m_t"]) * x["qk"]
        x["wv"] = _dot(w.astype(BF), x["v_ext"])
    for u in units:
        x = U[u]
        h = x["h"]
        s_inter = jnp.exp(x["inter"] - x["m_t"])
        num = s_inter * x["qs"][:, 0:LANES] + x["wv"][:, 0:LANES]
        den = s_inter * x["qs"][:, LANES:2 * LANES] + x["wv"][:, LANES:2 * LANES]
        hout = num / jnp.maximum(jnp.abs(den), jnp.exp(-x["m_t"]))
        x["o_ref"][:, h * LANES:(h + 1) * LANES] = hout.astype(x["o_ref"].dtype)
    for u in units:
        x = U[u]
        a_row = x["g"] - x["b_row"] + x["ig_row"]
        m_new = jnp.maximum(x["g"] + x["m"], jnp.max(a_row, axis=1, keepdims=True))
        wk_row = jnp.exp(a_row - m_new)
        decay = jnp.exp(x["g"] + x["m"] - m_new)
        kw = (x["kT_h"] * wk_row).astype(BF)
        state_ref[x["d"], x["h"]] = (jnp.concatenate([decay, decay], axis=1) * x["st"]
                                     + _dot(kw, x["v_ext"]))
        m_ref[x["d"], x["h"]] = m_new


def _mlstm(proj, gates, B, S):
    N = proj.shape[0]
    L = MLSTM_CHUNK
    nc = S // L
    H = MLSTM_HEADS
    t = np.arange(L)
    lower = (t[None, :] <= t[:, None]).astype(np.float32)
    tri = jnp.asarray(np.stack([lower, lower.T]), dtype=F32)
    sel_np = np.zeros((H, LANES, LANES), np.float32)
    for h in range(H):
        sel_np[h, H + h, :] = 1.0
    sel = jnp.asarray(sel_np, dtype=BF)

    fw = lambda b, c: b * nc + c
    bw = lambda b, c: b * nc + nc - 1 - c

    def chunk_specs(row):
        return [
            pl.BlockSpec((L, H * MLSTM_DK), lambda b, c: (row(b, c), 0)),
            pl.BlockSpec((L, H * MLSTM_DK), lambda b, c: (row(b, c), 1)),
            pl.BlockSpec((L, H * MLSTM_DV), lambda b, c: (row(b, c), 1)),
            pl.BlockSpec((L, LANES), lambda b, c: (row(b, c), 0)),
        ]

    return pl.pallas_call(
        _mlstm_kernel,
        grid=(B, nc),
        in_specs=chunk_specs(fw) + chunk_specs(bw) + [
            pl.BlockSpec((2, L, L), lambda b, c: (0, 0, 0)),
            pl.BlockSpec((H, LANES, LANES), lambda b, c: (0, 0, 0)),
        ],
        out_specs=[
            pl.BlockSpec((L, H * MLSTM_DV), lambda b, c: (fw(b, c), 0)),
            pl.BlockSpec((L, H * MLSTM_DV), lambda b, c: (bw(b, c), 0)),
        ],
        out_shape=[jax.ShapeDtypeStruct((N, H * MLSTM_DV), BF)] * 2,
        scratch_shapes=[
            pltpu.VMEM((2, H, 2 * MLSTM_DK, 2 * LANES), F32),
            pltpu.VMEM((2, H, 1, LANES), F32),
        ],
        compiler_params=_cp(("arbitrary", "arbitrary")),
        name="mlstm",
    )(proj, proj, proj, gates, proj, proj, proj, gates, tri, sel)


def _outproj_tail(a, x_ref, g1_ref, w_ref, n2_ref, sc2_ref, sh2_ref, wr_ref, xo_ref, h2_ref, lg_ref):
    xn = x_ref[...] + g1_ref[...] * _dot(a, w_ref[...])
    xo_ref[...] = xn
    h2 = _norm_mod(xn, n2_ref[...], sc2_ref[...], sh2_ref[...]).astype(BF)
    h2_ref[...] = h2
    lg = _dot_nt(wr_ref[...], h2)
    for cb in range(lg.shape[1] // LANES):
        lg_ref[cb] = lg[:, cb * LANES:(cb + 1) * LANES]


def _outproj_attn_kernel(a_ref, x_ref, g1_ref, w_ref, n2_ref, sc2_ref, sh2_ref, wr_ref,
                         xo_ref, h2_ref, lg_ref):
    _outproj_tail(a_ref[...], x_ref, g1_ref, w_ref, n2_ref, sc2_ref, sh2_ref, wr_ref,
                  xo_ref, h2_ref, lg_ref)


def _outproj_mlstm_kernel(hf_ref, hb_ref, og_ref, on_ref, bd_ref, x_ref, g1_ref, w_ref, n2_ref,
                          sc2_ref, sh2_ref, wr_ref, xo_ref, h2_ref, lg_ref, a_ref):
    for j in range(D_MODEL // 256):
        sl = slice(j * 256, (j + 1) * 256)
        hs = hf_ref[:, sl].astype(F32) + hb_ref[:, sl].astype(F32)
        ms = _group_sumsq(hs, bd_ref[...])
        hn = hs * (lax.rsqrt(ms + EPS) * on_ref[:, sl])
        a_ref[:, sl] = (hn * _sigmoid(og_ref[:, sl].astype(F32))).astype(BF)
    _outproj_tail(a_ref[...], x_ref, g1_ref, w_ref, n2_ref, sc2_ref, sh2_ref, wr_ref,
                  xo_ref, h2_ref, lg_ref)


def _outproj(kind, mix_inputs, x2, S, g1, w_bf, n2, sc2, sh2, wrT_bf, tm=512):
    N = x2.shape[0]
    spb = S // tm
    row = lambda i: (i, 0)
    const = lambda i: (0, 0)
    per_b = lambda i: (i // spb, 0, 0)
    tail_specs = [
        pl.BlockSpec((tm, D_MODEL), row),
        pl.BlockSpec((None, 1, D_MODEL), per_b),
        pl.BlockSpec((D_MODEL, D_MODEL), const),
        pl.BlockSpec((1, D_MODEL), const),
        pl.BlockSpec((None, 1, D_MODEL), per_b),
        pl.BlockSpec((None, 1, D_MODEL), per_b),
        pl.BlockSpec((N_EXPERTS, D_MODEL), const),
    ]
    out_specs = [
        pl.BlockSpec((tm, D_MODEL), row),
        pl.BlockSpec((tm, D_MODEL), row),
        pl.BlockSpec((tm // LANES, N_EXPERTS, LANES), lambda i: (i, 0, 0)),
    ]
    out_shape = [
        jax.ShapeDtypeStruct((N, D_MODEL), F32),
        jax.ShapeDtypeStruct((N, D_MODEL), BF),
        jax.ShapeDtypeStruct((N // LANES, N_EXPERTS, LANES), F32),
    ]
    tail_args = (x2, g1, w_bf, n2, sc2, sh2, wrT_bf)
    if kind == "attn":
        (a,) = mix_inputs
        return pl.pallas_call(
            _outproj_attn_kernel,
            grid=(N // tm,),
            in_specs=[pl.BlockSpec((tm, D_MODEL), row)] + tail_specs,
            out_specs=out_specs, out_shape=out_shape,
            compiler_params=_cp(("arbitrary",)),
            name="outproj_attn",
        )(a, *tail_args)
    h_fw, h_bw, proj, onorm_row = mix_inputs
    bd = _block_diag_ones(MLSTM_DV)
    return pl.pallas_call(
        _outproj_mlstm_kernel,
        grid=(N // tm,),
        in_specs=[
            pl.BlockSpec((tm, D_MODEL), row),
            pl.BlockSpec((tm, D_MODEL), row),
            pl.BlockSpec((tm, D_MODEL), lambda i: (i, 2)),
            pl.BlockSpec((1, D_MODEL), const),
            pl.BlockSpec((256, 256), const),
        ] + tail_specs,
        out_specs=out_specs, out_shape=out_shape,
        scratch_shapes=[pltpu.VMEM((tm, D_MODEL), BF)],
        compiler_params=_cp(("arbitrary",)),
        name="outproj_mlstm",
    )(h_fw, h_bw, proj, onorm_row, bd, *tail_args)


def _router_kernel(lg_ref, triu_ref, rank_ref, gate_ref, offs_ref, bits_ref, *, cap, blocks_per_tb):
    NB = lg_ref.shape[0]
    E = N_EXPERTS
    shape = (E, LANES)

    def softmax_body(b, carry):
        l = lg_ref[b]
        e = jnp.exp(l - jnp.max(l, axis=0, keepdims=True))
        aff = e / jnp.sum(e, axis=0, keepdims=True)
        gate_ref[b] = aff
        bits_ref[b] = pltpu.bitcast(aff, jnp.int32)
        return carry

    lax.fori_loop(0, NB, softmax_body, 0)

    def count(pred):
        def body(b, acc):
            return acc + jnp.where(pred(bits_ref[b]), 1.0, 0.0)
        acc = lax.fori_loop(0, NB, body, jnp.zeros(shape, F32))
        return jnp.broadcast_to(jnp.sum(acc, axis=1, keepdims=True), shape)

    def search_body(i, T):
        cand = T | jnp.left_shift(jnp.int32(1), 30 - i)
        cnt = count(lambda x: x >= cand)
        return jnp.where(cnt >= cap, cand, T)

    T = lax.fori_loop(0, 31, search_body, jnp.zeros(shape, jnp.int32))
    need = cap - count(lambda x: x > T)

    triu = triu_ref[...]
    ones = jnp.ones((LANES, LANES), BF)

    def tb_body(tb, carry):
        ceq, csel = carry
        offs_ref[tb] = csel
        for k in range(blocks_per_tb):
            b = tb * blocks_per_tb + k
            x = bits_ref[b]
            gt = x > T
            eq = x == T
            eqf = jnp.where(eq, 1.0, 0.0)
            eqb = eqf.astype(BF)
            rank_eq = ceq + _dot(eqb, triu) - eqf
            sel = gt | (eq & (rank_eq < need))
            self_ = jnp.where(sel, 1.0, 0.0)
            selb = self_.astype(BF)
            rank = csel + _dot(selb, triu) - self_
            rank_ref[b] = jnp.where(sel, rank, -1.0).astype(jnp.int32)
            gate_ref[b] = jnp.where(sel, gate_ref[b], 0.0)
            ceq = ceq + _dot(eqb, ones)
            csel = csel + _dot(selb, ones)
        return ceq, csel

    n_tb = NB // blocks_per_tb
    _, csel = lax.fori_loop(0, n_tb, tb_body, (jnp.zeros(shape, F32), jnp.zeros(shape, F32)))
    offs_ref[n_tb] = csel


def _router(lg, cap, tb_tokens):
    NB = lg.shape[0]
    bpt = tb_tokens // LANES
    n_tb = NB // bpt
    u = np.arange(LANES)
    triu = jnp.asarray((u[:, None] <= u[None, :]).astype(np.float32), dtype=BF)
    kern = functools.partial(_router_kernel, cap=float(cap), blocks_per_tb=bpt)
    return pl.pallas_call(
        kern,
        out_shape=[
            jax.ShapeDtypeStruct((NB, N_EXPERTS, LANES), jnp.int32),
            jax.ShapeDtypeStruct((NB, N_EXPERTS, LANES), F32),
            jax.ShapeDtypeStruct((n_tb + 1, N_EXPERTS, LANES), F32),
        ],
        scratch_shapes=[pltpu.VMEM((NB, N_EXPERTS, LANES), jnp.int32)],
        compiler_params=pltpu.CompilerParams(vmem_limit_bytes=VMEM_LIMIT),
        name="router",
    )(lg, triu)


def _schedule(offs, cap, slot_tile, token_major):
    E = N_EXPERTS
    n_t = offs.shape[0] - 1
    n_s = cap // slot_tile
    lo = offs[:-1]
    hi = offs[1:]
    nonempty = hi > lo
    s_lo = jnp.minimum(lo // slot_tile, n_s - 1)
    s_hi = jnp.where(nonempty, (hi - 1) // slot_tile, s_lo)
    cnt = jnp.where(nonempty, s_hi - s_lo + 1, 0)
    if token_major:
        cnt = cnt.at[:, 0].set(jnp.maximum(cnt[:, 0], 1))
        cnt_flat, slo_flat = cnt.reshape(-1), s_lo.reshape(-1)
        n_steps = E * (n_s + n_t) + n_t
    else:
        cnt_flat, slo_flat = cnt.T.reshape(-1), s_lo.T.reshape(-1)
        n_steps = E * (n_s + n_t)
    ends = jnp.cumsum(cnt_flat)
    total = ends[-1]
    i = jnp.arange(n_steps, dtype=jnp.int32)
    ic = jnp.minimum(i, total - 1)
    before = ends[None, :-1] <= ic[:, None]
    pair = jnp.sum(before, axis=1).astype(jnp.int32)
    val = slo_flat - (ends - cnt_flat)
    sv = val[0] + jnp.sum(jnp.where(before, (val[1:] - val[:-1])[None, :], 0), axis=1)
    s = (ic + sv).astype(jnp.int32)
    if token_major:
        t, e = pair // E, pair % E
        key = t
    else:
        e, t = pair // n_t, pair % n_t
        key = e * n_s + s
    valid = i < total
    change = key[1:] != key[:-1]
    first = valid & jnp.concatenate([jnp.ones((1,), bool), change])
    last = valid & jnp.concatenate([change | ~valid[1:], jnp.ones((1,), bool)])
    flags = first.astype(jnp.int32) + 2 * last.astype(jnp.int32) + 4 * valid.astype(jnp.int32)
    return e.astype(jnp.int32), s, t.astype(jnp.int32), flags


def _one_hot_slots(rank_ref, e, slot0, T, blocks):
    slot = slot0 + lax.broadcasted_iota(jnp.int32, (T, LANES), 0)
    pieces = [jnp.where(rank_ref[a, pl.ds(e, 1), :] == slot, 1.0, 0.0).astype(BF) for a in blocks]
    return jnp.concatenate(pieces, axis=1)


def _ffn_kernel(se_ref, ss_ref, st_ref, fl_ref, offs_ref, x_ref, rank_ref, gate_ref, wg_ref, wu_ref,
                wd_ref, ye_ref, xacc_ref, gacc_ref, *, T, sub, n_sub, f_chunk):
    i = pl.program_id(0)
    e = se_ref[i]
    s = ss_ref[i]
    fl = fl_ref[i]
    bps = sub // LANES

    @pl.when((fl & 1) != 0)
    def _():
        xacc_ref[...] = jnp.zeros(xacc_ref.shape, F32)
        gacc_ref[...] = jnp.zeros(gacc_ref.shape, F32)

    for a in range(n_sub):
        tsub = st_ref[i] * n_sub + a
        lo = offs_ref[tsub * N_EXPERTS + e]
        hi = offs_ref[(tsub + 1) * N_EXPERTS + e]

        @pl.when(((fl & 4) != 0) & (hi > s * T) & (lo < (s + 1) * T) & (hi > lo))
        def _(a=a):
            blocks = range(a * bps, (a + 1) * bps)
            P = _one_hot_slots(rank_ref, e, s * T, T, blocks)
            xacc_ref[...] += _dot(P, x_ref[a * sub:(a + 1) * sub, :])
            g = jnp.concatenate([gate_ref[b, pl.ds(e, 1), :] for b in blocks], axis=1)
            parts = [t.astype(F32) for t in _split3(g)]
            grows = jnp.concatenate(parts + [jnp.zeros((13, sub), F32)], axis=0).astype(BF)
            gacc_ref[...] += _dot_nt(P, grows)

    @pl.when((fl & 2) != 0)
    def _():
        x = xacc_ref[...].astype(BF)
        y = jnp.zeros((T, D_MODEL), F32)
        for c in range(EXPERT_FF // f_chunk):
            sl = slice(c * f_chunk, (c + 1) * f_chunk)
            gt = _dot(x, wg_ref[:, sl])
            up = _dot(x, wu_ref[:, sl])
            hid = (gt * _sigmoid(gt) * up).astype(BF)
            y = y + _dot(hid, wd_ref[sl, :])
        ga = gacc_ref[...]
        gcol = ga[:, 0:1] + ga[:, 1:2] + ga[:, 2:3]
        ye_ref[...] = (y * gcol).astype(ye_ref.dtype)


def _ffn(sched, offs_flat, h2, rankm, gate, wg, wu, wd, cap, T, sub, n_sub):
    se, ss, st, fl = sched
    n_steps = se.shape[0]
    TB = sub * n_sub
    nb = TB // LANES
    kern = functools.partial(_ffn_kernel, T=T, sub=sub, n_sub=n_sub, f_chunk=1024)
    grid_spec = pltpu.PrefetchScalarGridSpec(
        num_scalar_prefetch=5,
        grid=(n_steps,),
        in_specs=[
            pl.BlockSpec((TB, D_MODEL), lambda i, se, ss, st, fl, of: (st[i], 0)),
            pl.BlockSpec((nb, N_EXPERTS, LANES), lambda i, se, ss, st, fl, of: (st[i], 0, 0)),
            pl.BlockSpec((nb, N_EXPERTS, LANES), lambda i, se, ss, st, fl, of: (st[i], 0, 0)),
            pl.BlockSpec((None, D_MODEL, EXPERT_FF), lambda i, se, ss, st, fl, of: (se[i], 0, 0)),
            pl.BlockSpec((None, D_MODEL, EXPERT_FF), lambda i, se, ss, st, fl, of: (se[i], 0, 0)),
            pl.BlockSpec((None, EXPERT_FF, D_MODEL), lambda i, se, ss, st, fl, of: (se[i], 0, 0)),
        ],
        out_specs=pl.BlockSpec((None, T, D_MODEL), lambda i, se, ss, st, fl, of: (se[i], ss[i], 0)),
        scratch_shapes=[pltpu.VMEM((T, D_MODEL), F32), pltpu.VMEM((T, 16), F32)],
    )
    return pl.pallas_call(
        kern,
        grid_spec=grid_spec,
        out_shape=jax.ShapeDtypeStruct((N_EXPERTS, cap, D_MODEL), BF),
        compiler_params=_cp(("arbitrary",)),
        name="moe_ffn",
    )(se, ss, st, fl, offs_flat, h2, rankm, gate, wg, wu, wd)


def _combine_kernel(se_ref, ss_ref, st_ref, fl_ref, offs_ref, ye_ref, rank_ref, x_ref, g2_ref, o_ref,
                    acc_ref, *, T, n_tiles, n_blocks, subs_per_tile):
    i = pl.program_id(0)
    e = se_ref[i]
    s = ss_ref[i]
    t = st_ref[i]
    fl = fl_ref[i]

    @pl.when((fl & 1) != 0)
    def _():
        acc_ref[...] = jnp.zeros(acc_ref.shape, F32)

    lo = offs_ref[t * subs_per_tile * N_EXPERTS + e]
    hi = offs_ref[(t + 1) * subs_per_tile * N_EXPERTS + e]
    for j in range(n_tiles):
        slot0 = (s * n_tiles + j) * T

        @pl.when(((fl & 4) != 0) & (hi > slot0) & (lo < slot0 + T) & (hi > lo))
        def _(j=j, slot0=slot0):
            P = _one_hot_slots(rank_ref, e, slot0, T, range(n_blocks))
            acc_ref[...] += _dot_tn(P, ye_ref[j * T:(j + 1) * T, :])

    @pl.when((fl & 2) != 0)
    def _():
        o_ref[...] = x_ref[...] + g2_ref[...] * acc_ref[...]


def _combine(sched, offs_flat, ye, rankm, x1, g2, S, T, n_tiles, TT, sub):
    se, ss, st, fl = sched
    n_steps = se.shape[0]
    N = x1.shape[0]
    nb = TT // LANES
    spb = S // TT
    YB = T * n_tiles
    kern = functools.partial(_combine_kernel, T=T, n_tiles=n_tiles, n_blocks=nb, subs_per_tile=TT // sub)
    grid_spec = pltpu.PrefetchScalarGridSpec(
        num_scalar_prefetch=5,
        grid=(n_steps,),
        in_specs=[
            pl.BlockSpec((None, YB, D_MODEL), lambda i, se, ss, st, fl, of: (se[i], ss[i], 0)),
            pl.BlockSpec((nb, N_EXPERTS, LANES), lambda i, se, ss, st, fl, of: (st[i], 0, 0)),
            pl.BlockSpec((TT, D_MODEL), lambda i, se, ss, st, fl, of: (st[i], 0)),
            pl.BlockSpec((None, 1, D_MODEL), lambda i, se, ss, st, fl, of: (st[i] // spb, 0, 0)),
        ],
        out_specs=pl.BlockSpec((TT, D_MODEL), lambda i, se, ss, st, fl, of: (st[i], 0)),
        scratch_shapes=[pltpu.VMEM((TT, D_MODEL), F32)],
    )
    return pl.pallas_call(
        kern,
        grid_spec=grid_spec,
        out_shape=jax.ShapeDtypeStruct((N, D_MODEL), F32),
        compiler_params=_cp(("arbitrary",)),
        name="moe_combine",
    )(se, ss, st, fl, offs_flat, ye, rankm, x1, g2)


MOE_T = 256
MOE_SUB = 512
FFN_SUBS = 4
COMBINE_TT = 1024
YE_TILES = 2


def _moe(h2, lg, x1, g2, S, wg, wu, wd):
    N = x1.shape[0]
    cap = (CAPACITY_FACTOR * N) // N_EXPERTS
    T = min(MOE_T, cap)
    sub = min(MOE_SUB, S)
    n_sub = min(FFN_SUBS, S // sub)
    TT = min(COMBINE_TT, S)
    n_tiles = min(YE_TILES, cap // T)
    rankm, gate, offs = _router(lg, cap, sub)
    offs_i = offs[:, :, 0].astype(jnp.int32)
    offs_flat = offs_i.reshape(-1)
    fsched = _schedule(offs_i[::n_sub], cap, T, token_major=False)
    csched = _schedule(offs_i[::TT // sub], cap, T * n_tiles, token_major=True)
    ye = _ffn(fsched, offs_flat, h2, rankm, gate, wg, wu, wd, cap, T, sub, n_sub)
    return _combine(csched, offs_flat, ye, rankm, x1, g2, S, T, n_tiles, TT, sub)


def _trunk(x, mods, P):
    B, S, _ = x.shape
    x2 = x.reshape(B * S, D_MODEL)
    for l in range(DEPTH):
        mod = mods[l]
        sh1, sc1, g1, sh2, sc2, g2 = [mod[:, j][:, None, :] for j in range(N_MOD)]
        n1 = P["norm_g"][l, 0][None, :]
        n2 = P["norm_g"][l, 1][None, :]
        j = l // 2
        if l % 2 == 0:
            lambda_init = 0.8 - 0.6 * math.exp(-0.3 * l)
            qkv = _inproj_attn(x2, S, n1, sc1, sh1, P["attn_w_in"][j], P["attn_gain"][j])
            a = _attention(qkv, B, S, P["slopes"], P["attn_dmax"][j], P["attn_lambda"][j],
                           P["attn_subln"][j][None, :], lambda_init, tq=1024 if S >= 8192 else 512)
            x1, h2, lg = _outproj("attn", (a,), x2, S, g1, P["attn_w_out"][j], n2, sc2, sh2,
                                  P["w_routerT"][l])
        else:
            proj, gates = _inproj_mlstm(x2, S, n1, sc1, sh1, P["mlstm_w_main"][j], P["mlstm_w_gate"][j],
                                        P["mlstm_b_gate"][j])
            h_fw, h_bw = _mlstm(proj, gates, B, S)
            x1, h2, lg = _outproj("mlstm", (h_fw, h_bw, proj, P["mlstm_out_norm"][j]), x2, S, g1,
                                  P["mlstm_w_out"][j], n2, sc2, sh2, P["w_routerT"][l])
        x2 = _moe(h2, lg, x1, g2, S, P["w_exp_gate"][l], P["w_exp_up"][l], P["w_exp_down"][l])
    return x2.reshape(B, S, D_MODEL)


def _prepare(norm_g, attn_w_in, attn_q_gain, attn_k_gain, attn_lambda, attn_subln, attn_w_out,
             mlstm_w_in, mlstm_b_gate, mlstm_out_norm, mlstm_w_out, w_router, w_exp_gate, w_exp_up,
             w_exp_down):
    H = ATTN_HEADS
    n_main = 2 * MLSTM_HEADS * MLSTM_DK + MLSTM_HEADS * MLSTM_DV + D_MODEL
    n_gate = 4 * MLSTM_HEADS
    qg = jnp.tile(attn_q_gain * (ATTN_HEAD_DIM ** -0.5 * _LOG2E), (1, 2 * H))
    kg = jnp.tile(attn_k_gain, (1, 2 * H))
    slopes = 2.0 ** (-8.0 * np.arange(1, H + 1) / H)
    assert all(np.frexp(slopes)[0] == 0.5), "ALiBi slopes must be powers of two for bf16-exact features"
    smax = (8.0 * _LOG2E * 1.02) * jnp.max(jnp.abs(attn_q_gain), axis=1) * jnp.max(jnp.abs(attn_k_gain), axis=1)
    dmax = jnp.ceil((152.0 + 2.0 * smax[:, None]) / jnp.asarray(slopes * _LOG2E, F32)[None, :])
    dmax = jnp.clip(dmax, 0.0, 2.0 ** 30).astype(jnp.int32)
    return {
        "attn_dmax": dmax,
        "norm_g": norm_g,
        "attn_w_in": attn_w_in.astype(BF),
        "attn_gain": jnp.concatenate([qg, kg], axis=1)[:, None, :],
        "attn_lambda": attn_lambda,
        "attn_subln": attn_subln,
        "attn_w_out": attn_w_out.astype(BF),
        "slopes": jnp.asarray(slopes, F32),
        "mlstm_w_main": mlstm_w_in[:, :, :n_main].astype(BF),
        "mlstm_w_gate": jnp.pad(mlstm_w_in[:, :, n_main:], ((0, 0), (0, 0), (0, LANES - n_gate))).astype(BF),
        "mlstm_b_gate": jnp.pad(mlstm_b_gate, ((0, 0), (0, LANES - n_gate)))[:, None, :],
        "mlstm_out_norm": jnp.tile(mlstm_out_norm, (1, MLSTM_HEADS))[:, None, :],
        "mlstm_w_out": mlstm_w_out.astype(BF),
        "w_routerT": jnp.swapaxes(w_router, 1, 2).astype(BF),
        "w_exp_gate": w_exp_gate.astype(BF),
        "w_exp_up": w_exp_up.astype(BF),
        "w_exp_down": w_exp_down.astype(BF),
    }


def kernel(x_prompt, x_sample, c_prompt, c_sample, norm_g, w_ada, b_ada, attn_w_in, attn_q_gain,
           attn_k_gain, attn_lambda, attn_subln, attn_w_out, mlstm_w_in, mlstm_b_gate, mlstm_out_norm,
           mlstm_w_out, w_router, w_exp_gate, w_exp_up, w_exp_down):
    P = _prepare(norm_g, attn_w_in, attn_q_gain, attn_k_gain, attn_lambda, attn_subln, attn_w_out,
                 mlstm_w_in, mlstm_b_gate, mlstm_out_norm, mlstm_w_out, w_router, w_exp_gate,
                 w_exp_up, w_exp_down)
    bp, bs = c_prompt.shape[0], c_sample.shape[0]
    pad = (-(bp + bs)) % 8
    c_all = jnp.concatenate([c_prompt, c_sample, jnp.zeros((pad, D_MODEL), F32)], axis=0)
    mods = _ada(c_all, w_ada, b_ada)
    mods = mods.reshape(DEPTH, c_all.shape[0], N_MOD, D_MODEL)
    y_prompt = _trunk(x_prompt, mods[:, :bp], P)
    y_sample = _trunk(x_sample, mods[:, bp:bp + bs], P)
    return (y_prompt, y_sample)
```

```python
import functools
import math

import jax
import jax.numpy as jnp
import numpy as np
from jax import lax
from jax.experimental import pallas as pl
from jax.experimental.pallas import tpu as pltpu

D_MODEL = 1024
DEPTH = 4
ATTN_HEADS = 8
ATTN_HEAD_DIM = 64
MLSTM_HEADS = 8
MLSTM_DK = 64
MLSTM_DV = 128
MLSTM_CHUNK = 128
N_EXPERTS = 16
CAPACITY_FACTOR = 2
EXPERT_FF = 2048
N_MOD = 6
EPS = 1e-6

LANES = 128
VMEM_LIMIT = 56 * 1024 * 1024

BF = jnp.bfloat16
F32 = jnp.float32
NEG = -1e30


def _cp(sem, vmem=VMEM_LIMIT):
    return pltpu.CompilerParams(dimension_semantics=sem, vmem_limit_bytes=vmem)


def _sigmoid(x):
    return 1.0 / (1.0 + jnp.exp(-x))


def _dot(a, b):
    return jnp.dot(a, b, preferred_element_type=F32)


def _dot_nt(a, b):
    return lax.dot_general(a, b, (((1,), (1,)), ((), ())), preferred_element_type=F32)


def _dot_tn(a, b):
    return lax.dot_general(a, b, (((0,), (0,)), ((), ())), preferred_element_type=F32)


def _split3(x):
    a = x.astype(BF)
    r = x - a.astype(F32)
    b = r.astype(BF)
    c = (r - b.astype(F32)).astype(BF)
    return a, b, c


def _ada_kernel(c_ref, w_ref, b_ref, o_ref):
    c = c_ref[...]
    s = (c * _sigmoid(c)).astype(BF)
    o_ref[...] = _dot(s, w_ref[...].astype(BF)) + b_ref[...]


def _ada(c_all, w_ada, b_ada):
    bp = c_all.shape[0]
    n_out = N_MOD * D_MODEL
    tn = 1536
    return pl.pallas_call(
        _ada_kernel,
        grid=(DEPTH, n_out // tn),
        in_specs=[
            pl.BlockSpec((bp, D_MODEL), lambda l, j: (0, 0)),
            pl.BlockSpec((None, D_MODEL, tn), lambda l, j: (l, 0, j)),
            pl.BlockSpec((None, 1, tn), lambda l, j: (l, 0, j)),
        ],
        out_specs=pl.BlockSpec((None, bp, tn), lambda l, j: (l, 0, j)),
        out_shape=jax.ShapeDtypeStruct((DEPTH, bp, n_out), F32),
        compiler_params=_cp(("arbitrary", "arbitrary")),
        name="ada",
    )(c_all, w_ada, b_ada.reshape(DEPTH, 1, n_out))


def _norm_mod(x, g, sc, sh):
    r = lax.rsqrt(jnp.mean(x * x, axis=-1, keepdims=True) + EPS)
    return x * r * g * (1.0 + sc) + sh


def _group_sumsq(y, bd):
    return _dot((y * y).astype(BF), bd)


def _block_diag_ones(group, size=256):
    assert group & (group - 1) == 0
    i = np.arange(size) // group
    return jnp.asarray((i[:, None] == i[None, :]).astype(np.float32) / group, dtype=BF)


def _inproj_attn_kernel(x_ref, g_ref, sc_ref, sh_ref, w_ref, gain_ref, bd_ref, o_ref):
    h = _norm_mod(x_ref[...], g_ref[...], sc_ref[...], sh_ref[...]).astype(BF)
    wide = 512
    n_norm = gain_ref.shape[1] // wide
    for j in range(o_ref.shape[1] // wide):
        yw = _dot(h, w_ref[:, j * wide:(j + 1) * wide])
        for c in range(wide // 256):
            sl = slice(j * wide + c * 256, j * wide + (c + 1) * 256)
            y = yw[:, c * 256:(c + 1) * 256]
            if j < n_norm:
                ms = _group_sumsq(y, bd_ref[...])
                y = y * (lax.rsqrt(ms + EPS) * gain_ref[:, sl])
            o_ref[:, sl] = y.astype(o_ref.dtype)


def _inproj_attn(x2, S, g, sc, sh, w_bf, gain_row, tm=512):
    N = x2.shape[0]
    n_out = w_bf.shape[1]
    spb = S // tm
    bd = _block_diag_ones(ATTN_HEAD_DIM)
    return pl.pallas_call(
        _inproj_attn_kernel,
        grid=(N // tm,),
        in_specs=[
            pl.BlockSpec((tm, D_MODEL), lambda i: (i, 0)),
            pl.BlockSpec((1, D_MODEL), lambda i: (0, 0)),
            pl.BlockSpec((None, 1, D_MODEL), lambda i: (i // spb, 0, 0)),
            pl.BlockSpec((None, 1, D_MODEL), lambda i: (i // spb, 0, 0)),
            pl.BlockSpec((D_MODEL, n_out), lambda i: (0, 0)),
            pl.BlockSpec((1, gain_row.shape[1]), lambda i: (0, 0)),
            pl.BlockSpec((256, 256), lambda i: (0, 0)),
        ],
        out_specs=pl.BlockSpec((tm, n_out), lambda i: (i, 0)),
        out_shape=jax.ShapeDtypeStruct((N, n_out), BF),
        compiler_params=_cp(("arbitrary",)),
        name="inproj_attn",
    )(x2, g, sc, sh, w_bf, gain_row, bd)


def _inproj_mlstm_kernel(x_ref, g_ref, sc_ref, sh_ref, w_ref, wg_ref, bg_ref, o_ref, og_ref):
    h = _norm_mod(x_ref[...], g_ref[...], sc_ref[...], sh_ref[...]).astype(BF)
    hk = MLSTM_HEADS * MLSTM_DK
    for j in range(o_ref.shape[1] // 256):
        sl = slice(j * 256, (j + 1) * 256)
        y = _dot(h, w_ref[:, sl])
        if hk <= j * 256 < 2 * hk:
            y = y * (MLSTM_DK ** -0.5)
        o_ref[:, sl] = y.astype(o_ref.dtype)
    og_ref[...] = _dot(h, wg_ref[...]) + bg_ref[...]


def _inproj_mlstm(x2, S, g, sc, sh, w_bf, wg_bf, bg_row, tm=512):
    N = x2.shape[0]
    n_out = w_bf.shape[1]
    spb = S // tm
    return pl.pallas_call(
        _inproj_mlstm_kernel,
        grid=(N // tm,),
        in_specs=[
            pl.BlockSpec((tm, D_MODEL), lambda i: (i, 0)),
            pl.BlockSpec((1, D_MODEL), lambda i: (0, 0)),
            pl.BlockSpec((None, 1, D_MODEL), lambda i: (i // spb, 0, 0)),
            pl.BlockSpec((None, 1, D_MODEL), lambda i: (i // spb, 0, 0)),
            pl.BlockSpec((D_MODEL, n_out), lambda i: (0, 0)),
            pl.BlockSpec((D_MODEL, LANES), lambda i: (0, 0)),
            pl.BlockSpec((1, LANES), lambda i: (0, 0)),
        ],
        out_specs=[
            pl.BlockSpec((tm, n_out), lambda i: (i, 0)),
            pl.BlockSpec((tm, LANES), lambda i: (i, 0)),
        ],
        out_shape=[
            jax.ShapeDtypeStruct((N, n_out), BF),
            jax.ShapeDtypeStruct((N, LANES), F32),
        ],
        compiler_params=_cp(("arbitrary",)),
        name="inproj_mlstm",
    )(x2, g, sc, sh, w_bf, wg_bf, bg_row)


_LOG2E = math.log2(math.e)
N_FEAT = 12


def _log2e_split():
    c = np.float32(_LOG2E)
    out = []
    for _ in range(3):
        t = np.asarray(c, dtype=BF).astype(np.float32)
        out.append(float(t))
        c = np.float32(c - t)
    return out


def _attn_tables(S):
    c = _log2e_split()
    pos = np.arange(S)
    kf = np.zeros((S, LANES), np.float32)
    qc = np.zeros((1, LANES), np.float32)
    for t in range(3):
        kf[:, t] = -128.0 * c[t]
        kf[:, 3 + t] = -c[t]
        kf[:, 6 + t] = pos // 128
        kf[:, 9 + t] = pos % 128
        qc[0, 6 + t] = 128.0 * c[t]
        qc[0, 9 + t] = c[t]
    return jnp.asarray(kf, dtype=BF), jnp.asarray(qc, dtype=F32)


def _attn_kernel(slopes_ref, dmax_ref, q_ref, k_ref, v_ref, kf_ref, qc_ref, lam_ref, subln_ref, o_ref,
                 qs_ref, m_ref, l_ref, acc_ref, *, tq, tk, S, lambda_init, group):
    h = pl.program_id(1)
    qi = pl.program_id(2)
    nk = S // tk
    slope = slopes_ref[h]
    r0 = qi * tq

    q = q_ref[...]
    lane = lax.broadcasted_iota(jnp.int32, (tq, LANES), 1)
    zero = jnp.zeros_like(q)
    q0 = jnp.where(lane < ATTN_HEAD_DIM, q, zero)
    q1 = jnp.where(lane >= ATTN_HEAD_DIM, q, zero)
    pos = r0 + lax.broadcasted_iota(jnp.int32, (tq, LANES), 0)
    hi = (pos >> 7).astype(F32)
    lo = (pos & 127).astype(F32)
    feat = jnp.where(lane < 3, hi, jnp.where(lane < 6, lo, qc_ref[...])) * slope
    for sgn, f in ((0, feat.astype(BF)), (1, (-feat).astype(BF))):
        qs_ref[sgn, 0:tq, 0:LANES] = q0
        qs_ref[sgn, tq:2 * tq, 0:LANES] = q1
        qs_ref[sgn, 0:tq, LANES:2 * LANES] = f
        qs_ref[sgn, tq:2 * tq, LANES:2 * LANES] = f

    def scores(ki, sgn):
        off = pl.multiple_of(ki * tk, tk)
        kaug = jnp.concatenate([k_ref[pl.ds(off, tk), :], kf_ref[pl.ds(off, tk), :]], axis=1)
        return _dot_nt(qs_ref[sgn], kaug), off

    def weighted_values(p, off):
        v = v_ref[pl.ds(off, tk), :]
        pb = p.astype(BF)
        return jnp.concatenate([_dot(pb[0:tq], v), _dot(pb[tq:2 * tq], v)], axis=0)

    def diag_correction(off):
        row = lax.broadcasted_iota(jnp.int32, (tq, tk), 0)
        col = lax.broadcasted_iota(jnp.int32, (tq, tk), 1)
        corr = jnp.minimum(row - col + (r0 - off), 0).astype(F32) * (2.0 * _LOG2E * slope)
        return jnp.concatenate([corr, corr], axis=0)

    def chunk(ki, sgn, state, diagonal=False):
        m_prev, l_prev, acc_prev = state
        s, off = scores(ki, sgn)
        if diagonal:
            s = s + diag_correction(off)
        m_new = jnp.maximum(m_prev, jnp.max(s, axis=1, keepdims=True))
        p = jnp.exp2(s - jnp.concatenate([m_new] * (tk // LANES), axis=1))
        alpha = jnp.exp2(m_prev - m_new)
        l_new = alpha * l_prev + jnp.sum(p, axis=1, keepdims=True)
        acc_new = alpha * acc_prev + weighted_values(p, off)
        return m_new, l_new, acc_new

    def load_state():
        return m_ref[...], l_ref[...], acc_ref[...]

    def store_state(state):
        m_ref[...], l_ref[...], acc_ref[...] = state

    n_diag = max(1, tq // tk)
    kd = r0 // tk
    s, off = scores(kd, 0)
    s = s + diag_correction(off)
    m0 = jnp.max(s, axis=1, keepdims=True)
    p = jnp.exp2(s - m0)
    state = (jnp.broadcast_to(m0, m_ref.shape),
             jnp.broadcast_to(jnp.sum(p, axis=1, keepdims=True), l_ref.shape),
             weighted_values(p, off))
    for j in range(1, n_diag):
        state = chunk(kd + j, 0, state, diagonal=True)
    store_state(state)

    dmax = dmax_ref[h]
    k_lo = jnp.maximum(r0 - dmax, 0) // tk
    k_hi = jnp.minimum((r0 + tq - 1 + dmax) // tk, nk - 1)
    n_before = kd - k_lo
    n_off = n_before + (k_hi - (kd + n_diag - 1))

    def nth(i):
        after = (i >= n_before).astype(jnp.int32)
        return k_lo + i + after * n_diag, after

    def group_body(j, c):
        state = load_state()
        for u in range(group):
            state = chunk(*nth(group * j + u), state)
        store_state(state)
        return c

    def single_body(i, c):
        store_state(chunk(*nth(i), load_state()))
        return c

    n_groups = n_off // group
    lax.fori_loop(0, n_groups, group_body, 0)
    lax.fori_loop(n_groups * group, n_off, single_body, 0)


    lam = lam_ref[...]
    l01 = jnp.sum(lam[0:1, :] * lam[1:2, :], axis=1, keepdims=True)
    l23 = jnp.sum(lam[2:3, :] * lam[3:4, :], axis=1, keepdims=True)
    lmbda = jnp.exp(l01) - jnp.exp(l23) + lambda_init
    o0 = acc_ref[0:tq, :] / l_ref[0:tq, :]
    o1 = acc_ref[tq:2 * tq, :] / l_ref[tq:2 * tq, :]
    o = o0 - lmbda * o1
    r = lax.rsqrt(jnp.mean(o * o, axis=-1, keepdims=True) + EPS)
    o_ref[...] = (o * r * subln_ref[...] * (1.0 - lambda_init)).astype(o_ref.dtype)


def _attention(qkv, B, S, slopes, dmax, lam, subln_row, lambda_init, tq=512, tk=512, group=2):
    N = qkv.shape[0]
    H = ATTN_HEADS
    tq, tk = min(tq, S), min(tk, S)
    assert (tk % tq == 0 or tq % tk == 0) and S % tk == 0 and S % tq == 0 and S <= 128 * 256
    nq = S // tq
    kf, qc = _attn_tables(S)
    kern = functools.partial(_attn_kernel, tq=tq, tk=tk, S=S, lambda_init=lambda_init, group=group)
    grid_spec = pltpu.PrefetchScalarGridSpec(
        num_scalar_prefetch=2,
        grid=(B, H, nq),
        in_specs=[
            pl.BlockSpec((tq, LANES), lambda b, h, qi, sl, dm: (b * nq + qi, h)),
            pl.BlockSpec((S, LANES), lambda b, h, qi, sl, dm: (b, H + h)),
            pl.BlockSpec((S, LANES), lambda b, h, qi, sl, dm: (b, 2 * H + h)),
            pl.BlockSpec((S, LANES), lambda b, h, qi, sl, dm: (0, 0)),
            pl.BlockSpec((1, LANES), lambda b, h, qi, sl, dm: (0, 0)),
            pl.BlockSpec(lam.shape, lambda b, h, qi, sl, dm: (0, 0)),
            pl.BlockSpec((1, LANES), lambda b, h, qi, sl, dm: (0, 0)),
        ],
        out_specs=pl.BlockSpec((tq, LANES), lambda b, h, qi, sl, dm: (b * nq + qi, h)),
        scratch_shapes=[
            pltpu.VMEM((2, 2 * tq, 2 * LANES), BF),
            pltpu.VMEM((2 * tq, LANES), F32),
            pltpu.VMEM((2 * tq, LANES), F32),
            pltpu.VMEM((2 * tq, LANES), F32),
        ],
    )
    return pl.pallas_call(
        kern,
        grid_spec=grid_spec,
        out_shape=jax.ShapeDtypeStruct((N, D_MODEL), BF),
        compiler_params=_cp(("arbitrary", "arbitrary", "arbitrary")),
        name="diff_attn",
    )(slopes, dmax, qkv, qkv, qkv, kf, qc, lam, subln_row)


def _log_sigmoid(x):
    return jnp.minimum(x, 0.0) - jnp.log(1.0 + jnp.exp(-jnp.abs(x)))


def _mlstm_kernel(qf_ref, kf_ref, vf_ref, gf_ref, qb_ref, kb_ref, vb_ref, gb_ref, tri_ref, sel_ref,
                  of_ref, ob_ref, state_ref, m_ref):
    c = pl.program_id(1)
    L = MLSTM_CHUNK
    H = MLSTM_HEADS

    @pl.when(c == 0)
    def _():
        state_ref[...] = jnp.zeros(state_ref.shape, F32)
        m_ref[...] = jnp.zeros(m_ref.shape, F32)

    sub = lax.broadcasted_iota(jnp.int32, (2 * MLSTM_DK, L), 0)
    lane = lax.broadcasted_iota(jnp.int32, (L, LANES), 1)
    ones = jnp.ones((L, LANES), BF)

    units, U = [], {}
    for d, (q_ref, k_ref, v_ref, gate_ref, o_ref) in enumerate(
            ((qf_ref, kf_ref, vf_ref, gf_ref, of_ref), (qb_ref, kb_ref, vb_ref, gb_ref, ob_ref))):
        G = gate_ref[...]
        if d == 1:
            G = pltpu.roll(G, LANES - 2 * H, axis=1)
        A = jnp.where((lane >= H) & (lane < 2 * H), _log_sigmoid(G), G)
        AT = A.T
        tri_f = tri_ref[d]
        tri_d = tri_f.astype(BF)
        tri_o = tri_ref[1 - d].astype(BF)
        a1, a2, a3 = _split3(A)
        Bcol = _dot(tri_d, a1) + _dot(tri_d, a2) + _dot(tri_d, a3)
        t1, t2, t3 = _split3(AT)
        Brow = _dot(t1, tri_o) + _dot(t2, tri_o) + _dot(t3, tri_o)
        bsplit = _split3(Bcol)
        mask = tri_f > 0.0
        kT = k_ref[...].astype(F32).T
        q = q_ref[...]
        v = v_ref[...]
        for h in range(H):
            u = (d, h)
            units.append(u)
            own = (sub >= (h % 2) * MLSTM_DK) & (sub < (h % 2 + 1) * MLSTM_DK)
            U[u] = dict(
                d=d, h=h, o_ref=o_ref, mask=mask, bsplit=bsplit,
                ig_row=AT[h:h + 1, :], b_row=Brow[H + h:H + h + 1, :],
                q_pair=q[:, (h // 2) * LANES:(h // 2 + 1) * LANES],
                v_ext=jnp.concatenate([v[:, h * LANES:(h + 1) * LANES], ones], axis=1),
                kT_h=jnp.where(own, kT[(h // 2) * LANES:(h // 2 + 1) * LANES, :], 0.0),
                m=m_ref[d, h], st=state_ref[d, h])

    for u in units:
        x = U[u]
        sel = sel_ref[x["h"]]
        x["b_rep"] = sum(_dot(b, sel) for b in x["bsplit"])
        x["qk"] = _dot(x["q_pair"], x["kT_h"].astype(BF))
        x["qs"] = _dot(x["q_pair"], x["st"].astype(BF))
    for u in units:
        x = U[u]
        b_rep = x["b_rep"]
        x["g"] = b_rep[L - 1:L, :] if x["d"] == 0 else b_rep[0:1, :]
        x["Dm"] = jnp.where(x["mask"], b_rep - (x["b_row"] - x["ig_row"]), -jnp.inf)
        x["inter"] = b_rep + x["m"]
    for u in units:
        x = U[u]
        x["m_t"] = jnp.maximum(x["inter"], jnp.max(x["Dm"], axis=1, keepdims=True))
    for u in units:
        x = U[u]
        w = jnp.exp(x["Dm"] - x["m_t"]) * x["qk"]
        x["wv"] = _dot(w.astype(BF), x["v_ext"])
    for u in units:
        x = U[u]
        h = x["h"]
        s_inter = jnp.exp(x["inter"] - x["m_t"])
        num = s_inter * x["qs"][:, 0:LANES] + x["wv"][:, 0:LANES]
        den = s_inter * x["qs"][:, LANES:2 * LANES] + x["wv"][:, LANES:2 * LANES]
        hout = num / jnp.maximum(jnp.abs(den), jnp.exp(-x["m_t"]))
        x["o_ref"][:, h * LANES:(h + 1) * LANES] = hout.astype(x["o_ref"].dtype)
    for u in units:
        x = U[u]
        a_row = x["g"] - x["b_row"] + x["ig_row"]
        m_new = jnp.maximum(x["g"] + x["m"], jnp.max(a_row, axis=1, keepdims=True))
        wk_row = jnp.exp(a_row - m_new)
        decay = jnp.exp(x["g"] + x["m"] - m_new)
        kw = (x["kT_h"] * wk_row).astype(BF)
        state_ref[x["d"], x["h"]] = (jnp.concatenate([decay, decay], axis=1) * x["st"]
                                     + _dot(kw, x["v_ext"]))
        m_ref[x["d"], x["h"]] = m_new


def _mlstm(proj, gates, B, S):
    N = proj.shape[0]
    L = MLSTM_CHUNK
    nc = S // L
    H = MLSTM_HEADS
    t = np.arange(L)
    lower = (t[None, :] <= t[:, None]).astype(np.float32)
    tri = jnp.asarray(np.stack([lower, lower.T]), dtype=F32)
    sel_np = np.zeros((H, LANES, LANES), np.float32)
    for h in range(H):
        sel_np[h, H + h, :] = 1.0
    sel = jnp.asarray(sel_np, dtype=BF)

    fw = lambda b, c: b * nc + c
    bw = lambda b, c: b * nc + nc - 1 - c

    def chunk_specs(row):
        return [
            pl.BlockSpec((L, H * MLSTM_DK), lambda b, c: (row(b, c), 0)),
            pl.BlockSpec((L, H * MLSTM_DK), lambda b, c: (row(b, c), 1)),
            pl.BlockSpec((L, H * MLSTM_DV), lambda b, c: (row(b, c), 1)),
            pl.BlockSpec((L, LANES), lambda b, c: (row(b, c), 0)),
        ]

    return pl.pallas_call(
        _mlstm_kernel,
        grid=(B, nc),
        in_specs=chunk_specs(fw) + chunk_specs(bw) + [
            pl.BlockSpec((2, L, L), lambda b, c: (0, 0, 0)),
            pl.BlockSpec((H, LANES, LANES), lambda b, c: (0, 0, 0)),
        ],
        out_specs=[
            pl.BlockSpec((L, H * MLSTM_DV), lambda b, c: (fw(b, c), 0)),
            pl.BlockSpec((L, H * MLSTM_DV), lambda b, c: (bw(b, c), 0)),
        ],
        out_shape=[jax.ShapeDtypeStruct((N, H * MLSTM_DV), BF)] * 2,
        scratch_shapes=[
            pltpu.VMEM((2, H, 2 * MLSTM_DK, 2 * LANES), F32),
            pltpu.VMEM((2, H, 1, LANES), F32),
        ],
        compiler_params=_cp(("arbitrary", "arbitrary")),
        name="mlstm",
    )(proj, proj, proj, gates, proj, proj, proj, gates, tri, sel)


def _outproj_tail(a, x_ref, g1_ref, w_ref, n2_ref, sc2_ref, sh2_ref, wr_ref, xo_ref, h2_ref, lg_ref):
    xn = x_ref[...] + g1_ref[...] * _dot(a, w_ref[...])
    xo_ref[...] = xn
    h2 = _norm_mod(xn, n2_ref[...], sc2_ref[...], sh2_ref[...]).astype(BF)
    h2_ref[...] = h2
    lg = _dot_nt(wr_ref[...], h2)
    for cb in range(lg.shape[1] // LANES):
        lg_ref[cb] = lg[:, cb * LANES:(cb + 1) * LANES]


def _outproj_attn_kernel(a_ref, x_ref, g1_ref, w_ref, n2_ref, sc2_ref, sh2_ref, wr_ref,
                         xo_ref, h2_ref, lg_ref):
    _outproj_tail(a_ref[...], x_ref, g1_ref, w_ref, n2_ref, sc2_ref, sh2_ref, wr_ref,
                  xo_ref, h2_ref, lg_ref)


def _outproj_mlstm_kernel(hf_ref, hb_ref, og_ref, on_ref, bd_ref, x_ref, g1_ref, w_ref, n2_ref,
                          sc2_ref, sh2_ref, wr_ref, xo_ref, h2_ref, lg_ref, a_ref):
    for j in range(D_MODEL // 256):
        sl = slice(j * 256, (j + 1) * 256)
        hs = hf_ref[:, sl].astype(F32) + hb_ref[:, sl].astype(F32)
        ms = _group_sumsq(hs, bd_ref[...])
        hn = hs * (lax.rsqrt(ms + EPS) * on_ref[:, sl])
        a_ref[:, sl] = (hn * _sigmoid(og_ref[:, sl].astype(F32))).astype(BF)
    _outproj_tail(a_ref[...], x_ref, g1_ref, w_ref, n2_ref, sc2_ref, sh2_ref, wr_ref,
                  xo_ref, h2_ref, lg_ref)


def _outproj(kind, mix_inputs, x2, S, g1, w_bf, n2, sc2, sh2, wrT_bf, tm=512):
    N = x2.shape[0]
    spb = S // tm
    row = lambda i: (i, 0)
    const = lambda i: (0, 0)
    per_b = lambda i: (i // spb, 0, 0)
    tail_specs = [
        pl.BlockSpec((tm, D_MODEL), row),
        pl.BlockSpec((None, 1, D_MODEL), per_b),
        pl.BlockSpec((D_MODEL, D_MODEL), const),
        pl.BlockSpec((1, D_MODEL), const),
        pl.BlockSpec((None, 1, D_MODEL), per_b),
        pl.BlockSpec((None, 1, D_MODEL), per_b),
        pl.BlockSpec((N_EXPERTS, D_MODEL), const),
    ]
    out_specs = [
        pl.BlockSpec((tm, D_MODEL), row),
        pl.BlockSpec((tm, D_MODEL), row),
        pl.BlockSpec((tm // LANES, N_EXPERTS, LANES), lambda i: (i, 0, 0)),
    ]
    out_shape = [
        jax.ShapeDtypeStruct((N, D_MODEL), F32),
        jax.ShapeDtypeStruct((N, D_MODEL), BF),
        jax.ShapeDtypeStruct((N // LANES, N_EXPERTS, LANES), F32),
    ]
    tail_args = (x2, g1, w_bf, n2, sc2, sh2, wrT_bf)
    if kind == "attn":
        (a,) = mix_inputs
        return pl.pallas_call(
            _outproj_attn_kernel,
            grid=(N // tm,),
            in_specs=[pl.BlockSpec((tm, D_MODEL), row)] + tail_specs,
            out_specs=out_specs, out_shape=out_shape,
            compiler_params=_cp(("arbitrary",)),
            name="outproj_attn",
        )(a, *tail_args)
    h_fw, h_bw, proj, onorm_row = mix_inputs
    bd = _block_diag_ones(MLSTM_DV)
    return pl.pallas_call(
        _outproj_mlstm_kernel,
        grid=(N // tm,),
        in_specs=[
            pl.BlockSpec((tm, D_MODEL), row),
            pl.BlockSpec((tm, D_MODEL), row),
            pl.BlockSpec((tm, D_MODEL), lambda i: (i, 2)),
            pl.BlockSpec((1, D_MODEL), const),
            pl.BlockSpec((256, 256), const),
        ] + tail_specs,
        out_specs=out_specs, out_shape=out_shape,
        scratch_shapes=[pltpu.VMEM((tm, D_MODEL), BF)],
        compiler_params=_cp(("arbitrary",)),
        name="outproj_mlstm",
    )(h_fw, h_bw, proj, onorm_row, bd, *tail_args)


def _router_kernel(lg_ref, triu_ref, rank_ref, gate_ref, offs_ref, bits_ref, *, cap, blocks_per_tb):
    NB = lg_ref.shape[0]
    E = N_EXPERTS
    shape = (E, LANES)

    def softmax_body(b, carry):
        l = lg_ref[b]
        e = jnp.exp(l - jnp.max(l, axis=0, keepdims=True))
        aff = e / jnp.sum(e, axis=0, keepdims=True)
        gate_ref[b] = aff
        bits_ref[b] = pltpu.bitcast(aff, jnp.int32)
        return carry

    lax.fori_loop(0, NB, softmax_body, 0)

    def count(pred):
        def body(b, acc):
            return acc + jnp.where(pred(bits_ref[b]), 1.0, 0.0)
        acc = lax.fori_loop(0, NB, body, jnp.zeros(shape, F32))
        return jnp.broadcast_to(jnp.sum(acc, axis=1, keepdims=True), shape)

    def search_body(i, T):
        cand = T | jnp.left_shift(jnp.int32(1), 30 - i)
        cnt = count(lambda x: x >= cand)
        return jnp.where(cnt >= cap, cand, T)

    T = lax.fori_loop(0, 31, search_body, jnp.zeros(shape, jnp.int32))
    need = cap - count(lambda x: x > T)

    triu = triu_ref[...]
    ones = jnp.ones((LANES, LANES), BF)

    def tb_body(tb, carry):
        ceq, csel = carry
        offs_ref[tb] = csel
        for k in range(blocks_per_tb):
            b = tb * blocks_per_tb + k
            x = bits_ref[b]
            gt = x > T
            eq = x == T
            eqf = jnp.where(eq, 1.0, 0.0)
            eqb = eqf.astype(BF)
            rank_eq = ceq + _dot(eqb, triu) - eqf
            sel = gt | (eq & (rank_eq < need))
            self_ = jnp.where(sel, 1.0, 0.0)
            selb = self_.astype(BF)
            rank = csel + _dot(selb, triu) - self_
            rank_ref[b] = jnp.where(sel, rank, -1.0).astype(jnp.int32)
            gate_ref[b] = jnp.where(sel, gate_ref[b], 0.0)
            ceq = ceq + _dot(eqb, ones)
            csel = csel + _dot(selb, ones)
        return ceq, csel

    n_tb = NB // blocks_per_tb
    _, csel = lax.fori_loop(0, n_tb, tb_body, (jnp.zeros(shape, F32), jnp.zeros(shape, F32)))
    offs_ref[n_tb] = csel


def _router(lg, cap, tb_tokens):
    NB = lg.shape[0]
    bpt = tb_tokens // LANES
    n_tb = NB // bpt
    u = np.arange(LANES)
    triu = jnp.asarray((u[:, None] <= u[None, :]).astype(np.float32), dtype=BF)
    kern = functools.partial(_router_kernel, cap=float(cap), blocks_per_tb=bpt)
    return pl.pallas_call(
        kern,
        out_shape=[
            jax.ShapeDtypeStruct((NB, N_EXPERTS, LANES), jnp.int32),
            jax.ShapeDtypeStruct((NB, N_EXPERTS, LANES), F32),
            jax.ShapeDtypeStruct((n_tb + 1, N_EXPERTS, LANES), F32),
        ],
        scratch_shapes=[pltpu.VMEM((NB, N_EXPERTS, LANES), jnp.int32)],
        compiler_params=pltpu.CompilerParams(vmem_limit_bytes=VMEM_LIMIT),
        name="router",
    )(lg, triu)


def _schedule(offs, cap, slot_tile, token_major):
    E = N_EXPERTS
    n_t = offs.shape[0] - 1
    n_s = cap // slot_tile
    lo = offs[:-1]
    hi = offs[1:]
    nonempty = hi > lo
    s_lo = jnp.minimum(lo // slot_tile, n_s - 1)
    s_hi = jnp.where(nonempty, (hi - 1) // slot_tile, s_lo)
    cnt = jnp.where(nonempty, s_hi - s_lo + 1, 0)
    if token_major:
        cnt = cnt.at[:, 0].set(jnp.maximum(cnt[:, 0], 1))
        cnt_flat, slo_flat = cnt.reshape(-1), s_lo.reshape(-1)
        n_steps = E * (n_s + n_t) + n_t
    else:
        cnt_flat, slo_flat = cnt.T.reshape(-1), s_lo.T.reshape(-1)
        n_steps = E * (n_s + n_t)
    ends = jnp.cumsum(cnt_flat)
    total = ends[-1]
    i = jnp.arange(n_steps, dtype=jnp.int32)
    ic = jnp.minimum(i, total - 1)
    before = ends[None, :-1] <= ic[:, None]
    pair = jnp.sum(before, axis=1).astype(jnp.int32)
    val = slo_flat - (ends - cnt_flat)
    sv = val[0] + jnp.sum(jnp.where(before, (val[1:] - val[:-1])[None, :], 0), axis=1)
    s = (ic + sv).astype(jnp.int32)
    if token_major:
        t, e = pair // E, pair % E
        key = t
    else:
        e, t = pair // n_t, pair % n_t
        key = e * n_s + s
    valid = i < total
    change = key[1:] != key[:-1]
    first = valid & jnp.concatenate([jnp.ones((1,), bool), change])
    last = valid & jnp.concatenate([change | ~valid[1:], jnp.ones((1,), bool)])
    flags = first.astype(jnp.int32) + 2 * last.astype(jnp.int32) + 4 * valid.astype(jnp.int32)
    return e.astype(jnp.int32), s, t.astype(jnp.int32), flags


def _one_hot_slots(rank_ref, e, slot0, T, blocks):
    slot = slot0 + lax.broadcasted_iota(jnp.int32, (T, LANES), 0)
    pieces = [jnp.where(rank_ref[a, pl.ds(e, 1), :] == slot, 1.0, 0.0).astype(BF) for a in blocks]
    return jnp.concatenate(pieces, axis=1)


def _ffn_kernel(se_ref, ss_ref, st_ref, fl_ref, offs_ref, x_ref, rank_ref, gate_ref, wg_ref, wu_ref,
                wd_ref, ye_ref, xacc_ref, gacc_ref, *, T, sub, n_sub, f_chunk):
    i = pl.program_id(0)
    e = se_ref[i]
    s = ss_ref[i]
    fl = fl_ref[i]
    bps = sub // LANES

    @pl.when((fl & 1) != 0)
    def _():
        xacc_ref[...] = jnp.zeros(xacc_ref.shape, F32)
        gacc_ref[...] = jnp.zeros(gacc_ref.shape, F32)

    for a in range(n_sub):
        tsub = st_ref[i] * n_sub + a
        lo = offs_ref[tsub * N_EXPERTS + e]
        hi = offs_ref[(tsub + 1) * N_EXPERTS + e]

        @pl.when(((fl & 4) != 0) & (hi > s * T) & (lo < (s + 1) * T) & (hi > lo))
        def _(a=a):
            blocks = range(a * bps, (a + 1) * bps)
            P = _one_hot_slots(rank_ref, e, s * T, T, blocks)
            xacc_ref[...] += _dot(P, x_ref[a * sub:(a + 1) * sub, :])
            g = jnp.concatenate([gate_ref[b, pl.ds(e, 1), :] for b in blocks], axis=1)
            parts = [t.astype(F32) for t in _split3(g)]
            grows = jnp.concatenate(parts + [jnp.zeros((13, sub), F32)], axis=0).astype(BF)
            gacc_ref[...] += _dot_nt(P, grows)

    @pl.when((fl & 2) != 0)
    def _():
        x = xacc_ref[...].astype(BF)
        y = jnp.zeros((T, D_MODEL), F32)
        for c in range(EXPERT_FF // f_chunk):
            sl = slice(c * f_chunk, (c + 1) * f_chunk)
            gt = _dot(x, wg_ref[:, sl])
            up = _dot(x, wu_ref[:, sl])
            hid = (gt * _sigmoid(gt) * up).astype(BF)
            y = y + _dot(hid, wd_ref[sl, :])
        ga = gacc_ref[...]
        gcol = ga[:, 0:1] + ga[:, 1:2] + ga[:, 2:3]
        ye_ref[...] = (y * gcol).astype(ye_ref.dtype)


def _ffn(sched, offs_flat, h2, rankm, gate, wg, wu, wd, cap, T, sub, n_sub):
    se, ss, st, fl = sched
    n_steps = se.shape[0]
    TB = sub * n_sub
    nb = TB // LANES
    kern = functools.partial(_ffn_kernel, T=T, sub=sub, n_sub=n_sub, f_chunk=1024)
    grid_spec = pltpu.PrefetchScalarGridSpec(
        num_scalar_prefetch=5,
        grid=(n_steps,),
        in_specs=[
            pl.BlockSpec((TB, D_MODEL), lambda i, se, ss, st, fl, of: (st[i], 0)),
            pl.BlockSpec((nb, N_EXPERTS, LANES), lambda i, se, ss, st, fl, of: (st[i], 0, 0)),
            pl.BlockSpec((nb, N_EXPERTS, LANES), lambda i, se, ss, st, fl, of: (st[i], 0, 0)),
            pl.BlockSpec((None, D_MODEL, EXPERT_FF), lambda i, se, ss, st, fl, of: (se[i], 0, 0)),
            pl.BlockSpec((None, D_MODEL, EXPERT_FF), lambda i, se, ss, st, fl, of: (se[i], 0, 0)),
            pl.BlockSpec((None, EXPERT_FF, D_MODEL), lambda i, se, ss, st, fl, of: (se[i], 0, 0)),
        ],
        out_specs=pl.BlockSpec((None, T, D_MODEL), lambda i, se, ss, st, fl, of: (se[i], ss[i], 0)),
        scratch_shapes=[pltpu.VMEM((T, D_MODEL), F32), pltpu.VMEM((T, 16), F32)],
    )
    return pl.pallas_call(
        kern,
        grid_spec=grid_spec,
        out_shape=jax.ShapeDtypeStruct((N_EXPERTS, cap, D_MODEL), BF),
        compiler_params=_cp(("arbitrary",)),
        name="moe_ffn",
    )(se, ss, st, fl, offs_flat, h2, rankm, gate, wg, wu, wd)


def _combine_kernel(se_ref, ss_ref, st_ref, fl_ref, offs_ref, ye_ref, rank_ref, x_ref, g2_ref, o_ref,
                    acc_ref, *, T, n_tiles, n_blocks, subs_per_tile):
    i = pl.program_id(0)
    e = se_ref[i]
    s = ss_ref[i]
    t = st_ref[i]
    fl = fl_ref[i]

    @pl.when((fl & 1) != 0)
    def _():
        acc_ref[...] = jnp.zeros(acc_ref.shape, F32)

    lo = offs_ref[t * subs_per_tile * N_EXPERTS + e]
    hi = offs_ref[(t + 1) * subs_per_tile * N_EXPERTS + e]
    for j in range(n_tiles):
        slot0 = (s * n_tiles + j) * T

        @pl.when(((fl & 4) != 0) & (hi > slot0) & (lo < slot0 + T) & (hi > lo))
        def _(j=j, slot0=slot0):
            P = _one_hot_slots(rank_ref, e, slot0, T, range(n_blocks))
            acc_ref[...] += _dot_tn(P, ye_ref[j * T:(j + 1) * T, :])

    @pl.when((fl & 2) != 0)
    def _():
        o_ref[...] = x_ref[...] + g2_ref[...] * acc_ref[...]


def _combine(sched, offs_flat, ye, rankm, x1, g2, S, T, n_tiles, TT, sub):
    se, ss, st, fl = sched
    n_steps = se.shape[0]
    N = x1.shape[0]
    nb = TT // LANES
    spb = S // TT
    YB = T * n_tiles
    kern = functools.partial(_combine_kernel, T=T, n_tiles=n_tiles, n_blocks=nb, subs_per_tile=TT // sub)
    grid_spec = pltpu.PrefetchScalarGridSpec(
        num_scalar_prefetch=5,
        grid=(n_steps,),
        in_specs=[
            pl.BlockSpec((None, YB, D_MODEL), lambda i, se, ss, st, fl, of: (se[i], ss[i], 0)),
            pl.BlockSpec((nb, N_EXPERTS, LANES), lambda i, se, ss, st, fl, of: (st[i], 0, 0)),
            pl.BlockSpec((TT, D_MODEL), lambda i, se, ss, st, fl, of: (st[i], 0)),
            pl.BlockSpec((None, 1, D_MODEL), lambda i, se, ss, st, fl, of: (st[i] // spb, 0, 0)),
        ],
        out_specs=pl.BlockSpec((TT, D_MODEL), lambda i, se, ss, st, fl, of: (st[i], 0)),
        scratch_shapes=[pltpu.VMEM((TT, D_MODEL), F32)],
    )
    return pl.pallas_call(
        kern,
        grid_spec=grid_spec,
        out_shape=jax.ShapeDtypeStruct((N, D_MODEL), F32),
        compiler_params=_cp(("arbitrary",)),
        name="moe_combine",
    )(se, ss, st, fl, offs_flat, ye, rankm, x1, g2)


MOE_T = 256
MOE_SUB = 512
FFN_SUBS = 4
COMBINE_TT = 1024
YE_TILES = 2


def _moe(h2, lg, x1, g2, S, wg, wu, wd):
    N = x1.shape[0]
    cap = (CAPACITY_FACTOR * N) // N_EXPERTS
    T = min(MOE_T, cap)
    sub = min(MOE_SUB, S)
    n_sub = min(FFN_SUBS, S // sub)
    TT = min(COMBINE_TT, S)
    n_tiles = min(YE_TILES, cap // T)
    rankm, gate, offs = _router(lg, cap, sub)
    offs_i = offs[:, :, 0].astype(jnp.int32)
    offs_flat = offs_i.reshape(-1)
    fsched = _schedule(offs_i[::n_sub], cap, T, token_major=False)
    csched = _schedule(offs_i[::TT // sub], cap, T * n_tiles, token_major=True)
    ye = _ffn(fsched, offs_flat, h2, rankm, gate, wg, wu, wd, cap, T, sub, n_sub)
    return _combine(csched, offs_flat, ye, rankm, x1, g2, S, T, n_tiles, TT, sub)


def _trunk(x, mods, P):
    B, S, _ = x.shape
    x2 = x.reshape(B * S, D_MODEL)
    for l in range(DEPTH):
        mod = mods[l]
        sh1, sc1, g1, sh2, sc2, g2 = [mod[:, j][:, None, :] for j in range(N_MOD)]
        n1 = P["norm_g"][l, 0][None, :]
        n2 = P["norm_g"][l, 1][None, :]
        j = l // 2
        if l % 2 == 0:
            lambda_init = 0.8 - 0.6 * math.exp(-0.3 * l)
            qkv = _inproj_attn(x2, S, n1, sc1, sh1, P["attn_w_in"][j], P["attn_gain"][j])
            a = _attention(qkv, B, S, P["slopes"], P["attn_dmax"][j], P["attn_lambda"][j],
                           P["attn_subln"][j][None, :], lambda_init, tq=1024 if S >= 8192 else 512,
                           group=4 if S >= 8192 else 3)
            x1, h2, lg = _outproj("attn", (a,), x2, S, g1, P["attn_w_out"][j], n2, sc2, sh2,
                                  P["w_routerT"][l])
        else:
            proj, gates = _inproj_mlstm(x2, S, n1, sc1, sh1, P["mlstm_w_main"][j], P["mlstm_w_gate"][j],
                                        P["mlstm_b_gate"][j])
            h_fw, h_bw = _mlstm(proj, gates, B, S)
            x1, h2, lg = _outproj("mlstm", (h_fw, h_bw, proj, P["mlstm_out_norm"][j]), x2, S, g1,
                                  P["mlstm_w_out"][j], n2, sc2, sh2, P["w_routerT"][l])
        x2 = _moe(h2, lg, x1, g2, S, P["w_exp_gate"][l], P["w_exp_up"][l], P["w_exp_down"][l])
    return x2.reshape(B, S, D_MODEL)


def _prepare(norm_g, attn_w_in, attn_q_gain, attn_k_gain, attn_lambda, attn_subln, attn_w_out,
             mlstm_w_in, mlstm_b_gate, mlstm_out_norm, mlstm_w_out, w_router, w_exp_gate, w_exp_up,
             w_exp_down):
    H = ATTN_HEADS
    n_main = 2 * MLSTM_HEADS * MLSTM_DK + MLSTM_HEADS * MLSTM_DV + D_MODEL
    n_gate = 4 * MLSTM_HEADS
    qg = jnp.tile(attn_q_gain * (ATTN_HEAD_DIM ** -0.5 * _LOG2E), (1, 2 * H))
    kg = jnp.tile(attn_k_gain, (1, 2 * H))
    slopes = 2.0 ** (-8.0 * np.arange(1, H + 1) / H)
    assert all(np.frexp(slopes)[0] == 0.5), "ALiBi slopes must be powers of two for bf16-exact features"
    smax = (8.0 * _LOG2E * 1.02) * jnp.max(jnp.abs(attn_q_gain), axis=1) * jnp.max(jnp.abs(attn_k_gain), axis=1)
    dmax = jnp.ceil((152.0 + 2.0 * smax[:, None]) / jnp.asarray(slopes * _LOG2E, F32)[None, :])
    dmax = jnp.clip(dmax, 0.0, 2.0 ** 30).astype(jnp.int32)
    return {
        "attn_dmax": dmax,
        "norm_g": norm_g,
        "attn_w_in": attn_w_in.astype(BF),
        "attn_gain": jnp.concatenate([qg, kg], axis=1)[:, None, :],
        "attn_lambda": attn_lambda,
        "attn_subln": attn_subln,
        "attn_w_out": attn_w_out.astype(BF),
        "slopes": jnp.asarray(slopes, F32),
        "mlstm_w_main": mlstm_w_in[:, :, :n_main].astype(BF),
        "mlstm_w_gate": jnp.pad(mlstm_w_in[:, :, n_main:], ((0, 0), (0, 0), (0, LANES - n_gate))).astype(BF),
        "mlstm_b_gate": jnp.pad(mlstm_b_gate, ((0, 0), (0, LANES - n_gate)))[:, None, :],
        "mlstm_out_norm": jnp.tile(mlstm_out_norm, (1, MLSTM_HEADS))[:, None, :],
        "mlstm_w_out": mlstm_w_out.astype(BF),
        "w_routerT": jnp.swapaxes(w_router, 1, 2).astype(BF),
        "w_exp_gate": w_exp_gate.astype(BF),
        "w_exp_up": w_exp_up.astype(BF),
        "w_exp_down": w_exp_down.astype(BF),
    }


def kernel(x_prompt, x_sample, c_prompt, c_sample, norm_g, w_ada, b_ada, attn_w_in, attn_q_gain,
           attn_k_gain, attn_lambda, attn_subln, attn_w_out, mlstm_w_in, mlstm_b_gate, mlstm_out_norm,
           mlstm_w_out, w_router, w_exp_gate, w_exp_up, w_exp_down):
    P = _prepare(norm_g, attn_w_in, attn_q_gain, attn_k_gain, attn_lambda, attn_subln, attn_w_out,
                 mlstm_w_in, mlstm_b_gate, mlstm_out_norm, mlstm_w_out, w_router, w_exp_gate,
                 w_exp_up, w_exp_down)
    bp, bs = c_prompt.shape[0], c_sample.shape[0]
    pad = (-(bp + bs)) % 8
    c_all = jnp.concatenate([c_prompt, c_sample, jnp.zeros((pad, D_MODEL), F32)], axis=0)
    mods = _ada(c_all, w_ada, b_ada)
    mods = mods.reshape(DEPTH, c_all.shape[0], N_MOD, D_MODEL)
    y_prompt = _trunk(x_prompt, mods[:, :bp], P)
    y_sample = _trunk(x_sample, mods[:, bp:bp + bs], P)
    return (y_prompt, y_sample)
```

```python
import functools
import math

import jax
import jax.numpy as jnp
import numpy as np
from jax import lax
from jax.experimental import pallas as pl
from jax.experimental.pallas import tpu as pltpu

D_MODEL = 1024
DEPTH = 4
ATTN_HEADS = 8
ATTN_HEAD_DIM = 64
MLSTM_HEADS = 8
MLSTM_DK = 64
MLSTM_DV = 128
MLSTM_CHUNK = 128
N_EXPERTS = 16
CAPACITY_FACTOR = 2
EXPERT_FF = 2048
N_MOD = 6
EPS = 1e-6

LANES = 128
VMEM_LIMIT = 56 * 1024 * 1024

BF = jnp.bfloat16
F32 = jnp.float32
NEG = -1e30


def _cp(sem, vmem=VMEM_LIMIT):
    return pltpu.CompilerParams(dimension_semantics=sem, vmem_limit_bytes=vmem)


def _sigmoid(x):
    return 1.0 / (1.0 + jnp.exp(-x))


def _dot(a, b):
    return jnp.dot(a, b, preferred_element_type=F32)


def _dot_nt(a, b):
    return lax.dot_general(a, b, (((1,), (1,)), ((), ())), preferred_element_type=F32)


def _dot_tn(a, b):
    return lax.dot_general(a, b, (((0,), (0,)), ((), ())), preferred_element_type=F32)


def _split3(x):
    a = x.astype(BF)
    r = x - a.astype(F32)
    b = r.astype(BF)
    c = (r - b.astype(F32)).astype(BF)
    return a, b, c


def _ada_kernel(c_ref, w_ref, b_ref, o_ref):
    c = c_ref[...]
    s = (c * _sigmoid(c)).astype(BF)
    o_ref[...] = _dot(s, w_ref[...].astype(BF)) + b_ref[...]


def _ada(c_all, w_ada, b_ada):
    bp = c_all.shape[0]
    n_out = N_MOD * D_MODEL
    tn = 1536
    return pl.pallas_call(
        _ada_kernel,
        grid=(DEPTH, n_out // tn),
        in_specs=[
            pl.BlockSpec((bp, D_MODEL), lambda l, j: (0, 0)),
            pl.BlockSpec((None, D_MODEL, tn), lambda l, j: (l, 0, j)),
            pl.BlockSpec((None, 1, tn), lambda l, j: (l, 0, j)),
        ],
        out_specs=pl.BlockSpec((None, bp, tn), lambda l, j: (l, 0, j)),
        out_shape=jax.ShapeDtypeStruct((DEPTH, bp, n_out), F32),
        compiler_params=_cp(("arbitrary", "arbitrary")),
        name="ada",
    )(c_all, w_ada, b_ada.reshape(DEPTH, 1, n_out))


def _norm_mod(x, g, sc, sh):
    r = lax.rsqrt(jnp.mean(x * x, axis=-1, keepdims=True) + EPS)
    return x * r * g * (1.0 + sc) + sh


def _group_sumsq(y, bd):
    return _dot((y * y).astype(BF), bd)


def _block_diag_ones(group, size=256):
    assert group & (group - 1) == 0
    i = np.arange(size) // group
    return jnp.asarray((i[:, None] == i[None, :]).astype(np.float32) / group, dtype=BF)


def _inproj_attn_kernel(x_ref, g_ref, sc_ref, sh_ref, w_ref, gain_ref, bd_ref, o_ref):
    h = _norm_mod(x_ref[...], g_ref[...], sc_ref[...], sh_ref[...]).astype(BF)
    wide = 512
    n_norm = gain_ref.shape[1] // wide
    for j in range(o_ref.shape[1] // wide):
        yw = _dot(h, w_ref[:, j * wide:(j + 1) * wide])
        for c in range(wide // 256):
            sl = slice(j * wide + c * 256, j * wide + (c + 1) * 256)
            y = yw[:, c * 256:(c + 1) * 256]
            if j < n_norm:
                ms = _group_sumsq(y, bd_ref[...])
                y = y * (lax.rsqrt(ms + EPS) * gain_ref[:, sl])
            o_ref[:, sl] = y.astype(o_ref.dtype)


def _inproj_attn(x2, S, g, sc, sh, w_bf, gain_row, tm=512):
    N = x2.shape[0]
    n_out = w_bf.shape[1]
    spb = S // tm
    bd = _block_diag_ones(ATTN_HEAD_DIM)
    return pl.pallas_call(
        _inproj_attn_kernel,
        grid=(N // tm,),
        in_specs=[
            pl.BlockSpec((tm, D_MODEL), lambda i: (i, 0)),
            pl.BlockSpec((1, D_MODEL), lambda i: (0, 0)),
            pl.BlockSpec((None, 1, D_MODEL), lambda i: (i // spb, 0, 0)),
            pl.BlockSpec((None, 1, D_MODEL), lambda i: (i // spb, 0, 0)),
            pl.BlockSpec((D_MODEL, n_out), lambda i: (0, 0)),
            pl.BlockSpec((1, gain_row.shape[1]), lambda i: (0, 0)),
            pl.BlockSpec((256, 256), lambda i: (0, 0)),
        ],
        out_specs=pl.BlockSpec((tm, n_out), lambda i: (i, 0)),
        out_shape=jax.ShapeDtypeStruct((N, n_out), BF),
        compiler_params=_cp(("arbitrary",)),
        name="inproj_attn",
    )(x2, g, sc, sh, w_bf, gain_row, bd)


def _inproj_mlstm_kernel(x_ref, g_ref, sc_ref, sh_ref, w_ref, wg_ref, bg_ref, o_ref, og_ref):
    h = _norm_mod(x_ref[...], g_ref[...], sc_ref[...], sh_ref[...]).astype(BF)
    hk = MLSTM_HEADS * MLSTM_DK
    for j in range(o_ref.shape[1] // 256):
        sl = slice(j * 256, (j + 1) * 256)
        y = _dot(h, w_ref[:, sl])
        if hk <= j * 256 < 2 * hk:
            y = y * (MLSTM_DK ** -0.5)
        o_ref[:, sl] = y.astype(o_ref.dtype)
    og_ref[...] = _dot(h, wg_ref[...]) + bg_ref[...]


def _inproj_mlstm(x2, S, g, sc, sh, w_bf, wg_bf, bg_row, tm=512):
    N = x2.shape[0]
    n_out = w_bf.shape[1]
    spb = S // tm
    return pl.pallas_call(
        _inproj_mlstm_kernel,
        grid=(N // tm,),
        in_specs=[
            pl.BlockSpec((tm, D_MODEL), lambda i: (i, 0)),
            pl.BlockSpec((1, D_MODEL), lambda i: (0, 0)),
            pl.BlockSpec((None, 1, D_MODEL), lambda i: (i // spb, 0, 0)),
            pl.BlockSpec((None, 1, D_MODEL), lambda i: (i // spb, 0, 0)),
            pl.BlockSpec((D_MODEL, n_out), lambda i: (0, 0)),
            pl.BlockSpec((D_MODEL, LANES), lambda i: (0, 0)),
            pl.BlockSpec((1, LANES), lambda i: (0, 0)),
        ],
        out_specs=[
            pl.BlockSpec((tm, n_out), lambda i: (i, 0)),
            pl.BlockSpec((tm, LANES), lambda i: (i, 0)),
        ],
        out_shape=[
            jax.ShapeDtypeStruct((N, n_out), BF),
            jax.ShapeDtypeStruct((N, LANES), F32),
        ],
        compiler_params=_cp(("arbitrary",)),
        name="inproj_mlstm",
    )(x2, g, sc, sh, w_bf, wg_bf, bg_row)


_LOG2E = math.log2(math.e)
N_FEAT = 12


def _log2e_split():
    c = np.float32(_LOG2E)
    out = []
    for _ in range(3):
        t = np.asarray(c, dtype=BF).astype(np.float32)
        out.append(float(t))
        c = np.float32(c - t)
    return out


def _attn_tables(S):
    c = _log2e_split()
    pos = np.arange(S)
    kf = np.zeros((S, LANES), np.float32)
    qc = np.zeros((1, LANES), np.float32)
    for t in range(3):
        kf[:, t] = -128.0 * c[t]
        kf[:, 3 + t] = -c[t]
        kf[:, 6 + t] = pos // 128
        kf[:, 9 + t] = pos % 128
        qc[0, 6 + t] = 128.0 * c[t]
        qc[0, 9 + t] = c[t]
    return jnp.asarray(kf, dtype=BF), jnp.asarray(qc, dtype=F32)


def _attn_kernel(slopes_ref, dmax_ref, q_ref, k_ref, v_ref, kf_ref, qc_ref, lam_ref, subln_ref, o_ref,
                 qs_ref, m_ref, l_ref, acc_ref, *, tq, tk, S, lambda_init, group):
    h = pl.program_id(1)
    qi = pl.program_id(2)
    nk = S // tk
    slope = slopes_ref[h]
    r0 = qi * tq

    q = q_ref[...]
    lane = lax.broadcasted_iota(jnp.int32, (tq, LANES), 1)
    zero = jnp.zeros_like(q)
    q0 = jnp.where(lane < ATTN_HEAD_DIM, q, zero)
    q1 = jnp.where(lane >= ATTN_HEAD_DIM, q, zero)
    pos = r0 + lax.broadcasted_iota(jnp.int32, (tq, LANES), 0)
    hi = (pos >> 7).astype(F32)
    lo = (pos & 127).astype(F32)
    feat = jnp.where(lane < 3, hi, jnp.where(lane < 6, lo, qc_ref[...])) * slope
    for sgn, f in ((0, feat.astype(BF)), (1, (-feat).astype(BF))):
        qs_ref[sgn, 0:tq, 0:LANES] = q0
        qs_ref[sgn, tq:2 * tq, 0:LANES] = q1
        qs_ref[sgn, 0:tq, LANES:2 * LANES] = f
        qs_ref[sgn, tq:2 * tq, LANES:2 * LANES] = f

    def scores(ki, sgn):
        off = pl.multiple_of(ki * tk, tk)
        kaug = jnp.concatenate([k_ref[pl.ds(off, tk), :], kf_ref[pl.ds(off, tk), :]], axis=1)
        return _dot_nt(qs_ref[sgn], kaug), off

    def weighted_values(p, off):
        v = v_ref[pl.ds(off, tk), :]
        pb = p.astype(BF)
        return jnp.concatenate([_dot(pb[0:tq], v), _dot(pb[tq:2 * tq], v)], axis=0)

    def diag_correction(off):
        row = lax.broadcasted_iota(jnp.int32, (tq, tk), 0)
        col = lax.broadcasted_iota(jnp.int32, (tq, tk), 1)
        corr = jnp.minimum(row - col + (r0 - off), 0).astype(F32) * (2.0 * _LOG2E * slope)
        return jnp.concatenate([corr, corr], axis=0)

    def chunk(ki, sgn, state, diagonal=False):
        m_prev, l_prev, acc_prev = state
        s, off = scores(ki, sgn)
        if diagonal:
            s = s + diag_correction(off)
        m_new = jnp.maximum(m_prev, jnp.max(s, axis=1, keepdims=True))
        p = jnp.exp2(s - jnp.concatenate([m_new] * (tk // LANES), axis=1))
        alpha = jnp.exp2(m_prev - m_new)
        l_new = alpha * l_prev + jnp.sum(p, axis=1, keepdims=True)
        acc_new = alpha * acc_prev + weighted_values(p, off)
        return m_new, l_new, acc_new

    def load_state():
        return m_ref[...], l_ref[...], acc_ref[...]

    def store_state(state):
        m_ref[...], l_ref[...], acc_ref[...] = state

    n_diag = max(1, tq // tk)
    kd = r0 // tk
    s, off = scores(kd, 0)
    s = s + diag_correction(off)
    m0 = jnp.max(s, axis=1, keepdims=True)
    p = jnp.exp2(s - m0)
    state = (jnp.broadcast_to(m0, m_ref.shape),
             jnp.broadcast_to(jnp.sum(p, axis=1, keepdims=True), l_ref.shape),
             weighted_values(p, off))
    for j in range(1, n_diag):
        state = chunk(kd + j, 0, state, diagonal=True)
    store_state(state)

    dmax = dmax_ref[h]
    k_lo = jnp.maximum(r0 - dmax, 0) // tk
    k_hi = jnp.minimum((r0 + tq - 1 + dmax) // tk, nk - 1)
    n_before = kd - k_lo
    n_off = n_before + (k_hi - (kd + n_diag - 1))

    def nth(i):
        after = (i >= n_before).astype(jnp.int32)
        return k_lo + i + after * n_diag, after

    def group_body(j, c):
        state = load_state()
        for u in range(group):
            state = chunk(*nth(group * j + u), state)
        store_state(state)
        return c

    def single_body(i, c):
        store_state(chunk(*nth(i), load_state()))
        return c

    n_groups = n_off // group
    lax.fori_loop(0, n_groups, group_body, 0)
    lax.fori_loop(n_groups * group, n_off, single_body, 0)


    lam = lam_ref[...]
    l01 = jnp.sum(lam[0:1, :] * lam[1:2, :], axis=1, keepdims=True)
    l23 = jnp.sum(lam[2:3, :] * lam[3:4, :], axis=1, keepdims=True)
    lmbda = jnp.exp(l01) - jnp.exp(l23) + lambda_init
    o0 = acc_ref[0:tq, :] / l_ref[0:tq, :]
    o1 = acc_ref[tq:2 * tq, :] / l_ref[tq:2 * tq, :]
    o = o0 - lmbda * o1
    r = lax.rsqrt(jnp.mean(o * o, axis=-1, keepdims=True) + EPS)
    o_ref[...] = (o * r * subln_ref[...] * (1.0 - lambda_init)).astype(o_ref.dtype)


def _attention(qkv, B, S, slopes, dmax, lam, subln_row, lambda_init, tq=512, tk=512, group=2):
    N = qkv.shape[0]
    H = ATTN_HEADS
    tq, tk = min(tq, S), min(tk, S)
    assert (tk % tq == 0 or tq % tk == 0) and S % tk == 0 and S % tq == 0 and S <= 128 * 256
    nq = S // tq
    kf, qc = _attn_tables(S)
    kern = functools.partial(_attn_kernel, tq=tq, tk=tk, S=S, lambda_init=lambda_init, group=group)
    grid_spec = pltpu.PrefetchScalarGridSpec(
        num_scalar_prefetch=2,
        grid=(B, H, nq),
        in_specs=[
            pl.BlockSpec((tq, LANES), lambda b, h, qi, sl, dm: (b * nq + qi, h)),
            pl.BlockSpec((S, LANES), lambda b, h, qi, sl, dm: (b, H + h)),
            pl.BlockSpec((S, LANES), lambda b, h, qi, sl, dm: (b, 2 * H + h)),
            pl.BlockSpec((S, LANES), lambda b, h, qi, sl, dm: (0, 0)),
            pl.BlockSpec((1, LANES), lambda b, h, qi, sl, dm: (0, 0)),
            pl.BlockSpec(lam.shape, lambda b, h, qi, sl, dm: (0, 0)),
            pl.BlockSpec((1, LANES), lambda b, h, qi, sl, dm: (0, 0)),
        ],
        out_specs=pl.BlockSpec((tq, LANES), lambda b, h, qi, sl, dm: (b * nq + qi, h)),
        scratch_shapes=[
            pltpu.VMEM((2, 2 * tq, 2 * LANES), BF),
            pltpu.VMEM((2 * tq, LANES), F32),
            pltpu.VMEM((2 * tq, LANES), F32),
            pltpu.VMEM((2 * tq, LANES), F32),
        ],
    )
    return pl.pallas_call(
        kern,
        grid_spec=grid_spec,
        out_shape=jax.ShapeDtypeStruct((N, D_MODEL), BF),
        compiler_params=_cp(("arbitrary", "arbitrary", "arbitrary")),
        name="diff_attn",
    )(slopes, dmax, qkv, qkv, qkv, kf, qc, lam, subln_row)


def _log_sigmoid(x):
    return jnp.minimum(x, 0.0) - jnp.log(1.0 + jnp.exp(-jnp.abs(x)))


def _mlstm_kernel(qf_ref, kf_ref, vf_ref, gf_ref, qb_ref, kb_ref, vb_ref, gb_ref, tri_ref, sel_ref,
                  of_ref, ob_ref, state_ref, m_ref):
    c = pl.program_id(1)
    L = MLSTM_CHUNK
    H = MLSTM_HEADS

    @pl.when(c == 0)
    def _():
        state_ref[...] = jnp.zeros(state_ref.shape, F32)
        m_ref[...] = jnp.zeros(m_ref.shape, F32)

    sub = lax.broadcasted_iota(jnp.int32, (2 * MLSTM_DK, L), 0)
    lane = lax.broadcasted_iota(jnp.int32, (L, LANES), 1)
    ones = jnp.ones((L, LANES), BF)

    units, U = [], {}
    for d, (q_ref, k_ref, v_ref, gate_ref, o_ref) in enumerate(
            ((qf_ref, kf_ref, vf_ref, gf_ref, of_ref), (qb_ref, kb_ref, vb_ref, gb_ref, ob_ref))):
        G = gate_ref[...]
        if d == 1:
            G = pltpu.roll(G, LANES - 2 * H, axis=1)
        A = jnp.where((lane >= H) & (lane < 2 * H), _log_sigmoid(G), G)
        AT = A.T
        tri_f = tri_ref[d]
        tri_d = tri_f.astype(BF)
        tri_o = tri_ref[1 - d].astype(BF)
        a1, a2, a3 = _split3(A)
        Bcol = _dot(tri_d, a1) + _dot(tri_d, a2) + _dot(tri_d, a3)
        t1, t2, t3 = _split3(AT)
        Brow = _dot(t1, tri_o) + _dot(t2, tri_o) + _dot(t3, tri_o)
        bsplit = _split3(Bcol)
        mask = tri_f > 0.0
        kT = k_ref[...].astype(F32).T
        q = q_ref[...]
        v = v_ref[...]
        for h in range(H):
            u = (d, h)
            units.append(u)
            own = (sub >= (h % 2) * MLSTM_DK) & (sub < (h % 2 + 1) * MLSTM_DK)
            U[u] = dict(
                d=d, h=h, o_ref=o_ref, mask=mask, bsplit=bsplit,
                ig_row=AT[h:h + 1, :], b_row=Brow[H + h:H + h + 1, :],
                q_pair=q[:, (h // 2) * LANES:(h // 2 + 1) * LANES],
                v_ext=jnp.concatenate([v[:, h * LANES:(h + 1) * LANES], ones], axis=1),
                kT_h=jnp.where(own, kT[(h // 2) * LANES:(h // 2 + 1) * LANES, :], 0.0),
                m=m_ref[d, h], st=state_ref[d, h])

    for u in units:
        x = U[u]
        sel = sel_ref[x["h"]]
        x["b_rep"] = sum(_dot(b, sel) for b in x["bsplit"])
        x["qk"] = _dot(x["q_pair"], x["kT_h"].astype(BF))
        x["qs"] = _dot(x["q_pair"], x["st"].astype(BF))
    for u in units:
        x = U[u]
        b_rep = x["b_rep"]
        x["g"] = b_rep[L - 1:L, :] if x["d"] == 0 else b_rep[0:1, :]
        x["Dm"] = jnp.where(x["mask"], b_rep - (x["b_row"] - x["ig_row"]), -jnp.inf)
        x["inter"] = b_rep + x["m"]
    for u in units:
        x = U[u]
        x["m_t"] = jnp.maximum(x["inter"], jnp.max(x["Dm"], axis=1, keepdims=True))
    for u in units:
        x = U[u]
        w = jnp.exp(x["Dm"] - x["m_t"]) * x["qk"]
        x["wv"] = _dot(w.astype(BF), x["v_ext"])
    for u in units:
        x = U[u]
        h = x["h"]
        s_inter = jnp.exp(x["inter"] - x["m_t"])
        num = s_inter * x["qs"][:, 0:LANES] + x["wv"][:, 0:LANES]
        den = s_inter * x["qs"][:, LANES:2 * LANES] + x["wv"][:, LANES:2 * LANES]
        hout = num / jnp.maximum(jnp.abs(den), jnp.exp(-x["m_t"]))
        x["o_ref"][:, h * LANES:(h + 1) * LANES] = hout.astype(x["o_ref"].dtype)
    for u in units:
        x = U[u]
        a_row = x["g"] - x["b_row"] + x["ig_row"]
        m_new = jnp.maximum(x["g"] + x["m"], jnp.max(a_row, axis=1, keepdims=True))
        wk_row = jnp.exp(a_row - m_new)
        decay = jnp.exp(x["g"] + x["m"] - m_new)
        kw = (x["kT_h"] * wk_row).astype(BF)
        state_ref[x["d"], x["h"]] = (jnp.concatenate([decay, decay], axis=1) * x["st"]
                                     + _dot(kw, x["v_ext"]))
        m_ref[x["d"], x["h"]] = m_new


def _mlstm(proj, gates, B, S):
    N = proj.shape[0]
    L = MLSTM_CHUNK
    nc = S // L
    H = MLSTM_HEADS
    t = np.arange(L)
    lower = (t[None, :] <= t[:, None]).astype(np.float32)
    tri = jnp.asarray(np.stack([lower, lower.T]), dtype=F32)
    sel_np = np.zeros((H, LANES, LANES), np.float32)
    for h in range(H):
        sel_np[h, H + h, :] = 1.0
    sel = jnp.asarray(sel_np, dtype=BF)

    fw = lambda b, c: b * nc + c
    bw = lambda b, c: b * nc + nc - 1 - c

    def chunk_specs(row):
        return [
            pl.BlockSpec((L, H * MLSTM_DK), lambda b, c: (row(b, c), 0)),
            pl.BlockSpec((L, H * MLSTM_DK), lambda b, c: (row(b, c), 1)),
            pl.BlockSpec((L, H * MLSTM_DV), lambda b, c: (row(b, c), 1)),
            pl.BlockSpec((L, LANES), lambda b, c: (row(b, c), 0)),
        ]

    return pl.pallas_call(
        _mlstm_kernel,
        grid=(B, nc),
        in_specs=chunk_specs(fw) + chunk_specs(bw) + [
            pl.BlockSpec((2, L, L), lambda b, c: (0, 0, 0)),
            pl.BlockSpec((H, LANES, LANES), lambda b, c: (0, 0, 0)),
        ],
        out_specs=[
            pl.BlockSpec((L, H * MLSTM_DV), lambda b, c: (fw(b, c), 0)),
            pl.BlockSpec((L, H * MLSTM_DV), lambda b, c: (bw(b, c), 0)),
        ],
        out_shape=[jax.ShapeDtypeStruct((N, H * MLSTM_DV), BF)] * 2,
        scratch_shapes=[
            pltpu.VMEM((2, H, 2 * MLSTM_DK, 2 * LANES), F32),
            pltpu.VMEM((2, H, 1, LANES), F32),
        ],
        compiler_params=_cp(("arbitrary", "arbitrary")),
        name="mlstm",
    )(proj, proj, proj, gates, proj, proj, proj, gates, tri, sel)


def _outproj_tail(a, x_ref, g1_ref, w_ref, n2_ref, sc2_ref, sh2_ref, wr_ref, xo_ref, h2_ref, lg_ref):
    xn = x_ref[...] + g1_ref[...] * _dot(a, w_ref[...])
    xo_ref[...] = xn
    h2 = _norm_mod(xn, n2_ref[...], sc2_ref[...], sh2_ref[...]).astype(BF)
    h2_ref[...] = h2
    lg = _dot_nt(wr_ref[...], h2)
    for cb in range(lg.shape[1] // LANES):
        lg_ref[cb] = lg[:, cb * LANES:(cb + 1) * LANES]


def _outproj_attn_kernel(a_ref, x_ref, g1_ref, w_ref, n2_ref, sc2_ref, sh2_ref, wr_ref,
                         xo_ref, h2_ref, lg_ref):
    _outproj_tail(a_ref[...], x_ref, g1_ref, w_ref, n2_ref, sc2_ref, sh2_ref, wr_ref,
                  xo_ref, h2_ref, lg_ref)


def _outproj_mlstm_kernel(hf_ref, hb_ref, og_ref, on_ref, bd_ref, x_ref, g1_ref, w_ref, n2_ref,
                          sc2_ref, sh2_ref, wr_ref, xo_ref, h2_ref, lg_ref, a_ref):
    for j in range(D_MODEL // 256):
        sl = slice(j * 256, (j + 1) * 256)
        hs = hf_ref[:, sl].astype(F32) + hb_ref[:, sl].astype(F32)
        ms = _group_sumsq(hs, bd_ref[...])
        hn = hs * (lax.rsqrt(ms + EPS) * on_ref[:, sl])
        a_ref[:, sl] = (hn * _sigmoid(og_ref[:, sl].astype(F32))).astype(BF)
    _outproj_tail(a_ref[...], x_ref, g1_ref, w_ref, n2_ref, sc2_ref, sh2_ref, wr_ref,
                  xo_ref, h2_ref, lg_ref)


def _outproj(kind, mix_inputs, x2, S, g1, w_bf, n2, sc2, sh2, wrT_bf, tm=512):
    N = x2.shape[0]
    spb = S // tm
    row = lambda i: (i, 0)
    const = lambda i: (0, 0)
    per_b = lambda i: (i // spb, 0, 0)
    tail_specs = [
        pl.BlockSpec((tm, D_MODEL), row),
        pl.BlockSpec((None, 1, D_MODEL), per_b),
        pl.BlockSpec((D_MODEL, D_MODEL), const),
        pl.BlockSpec((1, D_MODEL), const),
        pl.BlockSpec((None, 1, D_MODEL), per_b),
        pl.BlockSpec((None, 1, D_MODEL), per_b),
        pl.BlockSpec((N_EXPERTS, D_MODEL), const),
    ]
    out_specs = [
        pl.BlockSpec((tm, D_MODEL), row),
        pl.BlockSpec((tm, D_MODEL), row),
        pl.BlockSpec((tm // LANES, N_EXPERTS, LANES), lambda i: (i, 0, 0)),
    ]
    out_shape = [
        jax.ShapeDtypeStruct((N, D_MODEL), F32),
        jax.ShapeDtypeStruct((N, D_MODEL), BF),
        jax.ShapeDtypeStruct((N // LANES, N_EXPERTS, LANES), F32),
    ]
    tail_args = (x2, g1, w_bf, n2, sc2, sh2, wrT_bf)
    if kind == "attn":
        (a,) = mix_inputs
        return pl.pallas_call(
            _outproj_attn_kernel,
            grid=(N // tm,),
            in_specs=[pl.BlockSpec((tm, D_MODEL), row)] + tail_specs,
            out_specs=out_specs, out_shape=out_shape,
            compiler_params=_cp(("arbitrary",)),
            name="outproj_attn",
        )(a, *tail_args)
    h_fw, h_bw, proj, onorm_row = mix_inputs
    bd = _block_diag_ones(MLSTM_DV)
    return pl.pallas_call(
        _outproj_mlstm_kernel,
        grid=(N // tm,),
        in_specs=[
            pl.BlockSpec((tm, D_MODEL), row),
            pl.BlockSpec((tm, D_MODEL), row),
            pl.BlockSpec((tm, D_MODEL), lambda i: (i, 2)),
            pl.BlockSpec((1, D_MODEL), const),
            pl.BlockSpec((256, 256), const),
        ] + tail_specs,
        out_specs=out_specs, out_shape=out_shape,
        scratch_shapes=[pltpu.VMEM((tm, D_MODEL), BF)],
        compiler_params=_cp(("arbitrary",)),
        name="outproj_mlstm",
    )(h_fw, h_bw, proj, onorm_row, bd, *tail_args)


def _router_kernel(lg_ref, triu_ref, rank_ref, gate_ref, offs_ref, bits_ref, *, cap, blocks_per_tb):
    NB = lg_ref.shape[0]
    E = N_EXPERTS
    shape = (E, LANES)

    def softmax_body(b, carry):
        l = lg_ref[b]
        e = jnp.exp(l - jnp.max(l, axis=0, keepdims=True))
        aff = e / jnp.sum(e, axis=0, keepdims=True)
        gate_ref[b] = aff
        bits_ref[b] = pltpu.bitcast(aff, jnp.int32)
        return carry

    lax.fori_loop(0, NB, softmax_body, 0)

    def count(pred):
        def body(b, acc):
            return acc + jnp.where(pred(bits_ref[b]), 1.0, 0.0)
        acc = lax.fori_loop(0, NB, body, jnp.zeros(shape, F32))
        return jnp.broadcast_to(jnp.sum(acc, axis=1, keepdims=True), shape)

    def search_body(i, T):
        cand = T | jnp.left_shift(jnp.int32(1), 30 - i)
        cnt = count(lambda x: x >= cand)
        return jnp.where(cnt >= cap, cand, T)

    T = lax.fori_loop(0, 31, search_body, jnp.zeros(shape, jnp.int32))
    need = cap - count(lambda x: x > T)

    triu = triu_ref[...]
    ones = jnp.ones((LANES, LANES), BF)

    def tb_body(tb, carry):
        ceq, csel = carry
        offs_ref[tb] = csel
        for k in range(blocks_per_tb):
            b = tb * blocks_per_tb + k
            x = bits_ref[b]
            gt = x > T
            eq = x == T
            eqf = jnp.where(eq, 1.0, 0.0)
            eqb = eqf.astype(BF)
            rank_eq = ceq + _dot(eqb, triu) - eqf
            sel = gt | (eq & (rank_eq < need))
            self_ = jnp.where(sel, 1.0, 0.0)
            selb = self_.astype(BF)
            rank = csel + _dot(selb, triu) - self_
            rank_ref[b] = jnp.where(sel, rank, -1.0).astype(jnp.int32)
            gate_ref[b] = jnp.where(sel, gate_ref[b], 0.0)
            ceq = ceq + _dot(eqb, ones)
            csel = csel + _dot(selb, ones)
        return ceq, csel

    n_tb = NB // blocks_per_tb
    _, csel = lax.fori_loop(0, n_tb, tb_body, (jnp.zeros(shape, F32), jnp.zeros(shape, F32)))
    offs_ref[n_tb] = csel


def _router(lg, cap, tb_tokens):
    NB = lg.shape[0]
    bpt = tb_tokens // LANES
    n_tb = NB // bpt
    u = np.arange(LANES)
    triu = jnp.asarray((u[:, None] <= u[None, :]).astype(np.float32), dtype=BF)
    kern = functools.partial(_router_kernel, cap=float(cap), blocks_per_tb=bpt)
    return pl.pallas_call(
        kern,
        out_shape=[
            jax.ShapeDtypeStruct((NB, N_EXPERTS, LANES), jnp.int32),
            jax.ShapeDtypeStruct((NB, N_EXPERTS, LANES), F32),
            jax.ShapeDtypeStruct((n_tb + 1, N_EXPERTS, LANES), F32),
        ],
        scratch_shapes=[pltpu.VMEM((NB, N_EXPERTS, LANES), jnp.int32)],
        compiler_params=pltpu.CompilerParams(vmem_limit_bytes=VMEM_LIMIT),
        name="router",
    )(lg, triu)


def _schedule(offs, cap, slot_tile, token_major):
    E = N_EXPERTS
    n_t = offs.shape[0] - 1
    n_s = cap // slot_tile
    lo = offs[:-1]
    hi = offs[1:]
    nonempty = hi > lo
    s_lo = jnp.minimum(lo // slot_tile, n_s - 1)
    s_hi = jnp.where(nonempty, (hi - 1) // slot_tile, s_lo)
    cnt = jnp.where(nonempty, s_hi - s_lo + 1, 0)
    if token_major:
        cnt = cnt.at[:, 0].set(jnp.maximum(cnt[:, 0], 1))
        cnt_flat, slo_flat = cnt.reshape(-1), s_lo.reshape(-1)
        n_steps = E * (n_s + n_t) + n_t
    else:
        cnt_flat, slo_flat = cnt.T.reshape(-1), s_lo.T.reshape(-1)
        n_steps = E * (n_s + n_t)
    ends = jnp.cumsum(cnt_flat)
    total = ends[-1]
    i = jnp.arange(n_steps, dtype=jnp.int32)
    ic = jnp.minimum(i, total - 1)
    before = ends[None, :-1] <= ic[:, None]
    pair = jnp.sum(before, axis=1).astype(jnp.int32)
    val = slo_flat - (ends - cnt_flat)
    sv = val[0] + jnp.sum(jnp.where(before, (val[1:] - val[:-1])[None, :], 0), axis=1)
    s = (ic + sv).astype(jnp.int32)
    if token_major:
        t, e = pair // E, pair % E
        key = t
    else:
        e, t = pair // n_t, pair % n_t
        key = e * n_s + s
    valid = i < total
    change = key[1:] != key[:-1]
    first = valid & jnp.concatenate([jnp.ones((1,), bool), change])
    last = valid & jnp.concatenate([change | ~valid[1:], jnp.ones((1,), bool)])
    flags = first.astype(jnp.int32) + 2 * last.astype(jnp.int32) + 4 * valid.astype(jnp.int32)
    return e.astype(jnp.int32), s, t.astype(jnp.int32), flags


def _one_hot_slots(rank_ref, e, slot0, T, blocks):
    slot = slot0 + lax.broadcasted_iota(jnp.int32, (T, LANES), 0)
    pieces = [jnp.where(rank_ref[a, pl.ds(e, 1), :] == slot, 1.0, 0.0).astype(BF) for a in blocks]
    return jnp.concatenate(pieces, axis=1)


FFN_RING = 3


def _ffn_kernel(se_ref, ss_ref, st_ref, fl_ref, offs_ref, fk_ref, fblk_ref, x_hbm, rank_ref, gate_ref,
                wg_ref, wu_ref, wd_ref, ye_ref, xacc_ref, gacc_ref, xbuf_ref, sem_ref,
                *, T, sub, n_sub, f_chunk):
    i = pl.program_id(0)
    e = se_ref[i]
    s = ss_ref[i]
    fl = fl_ref[i]
    bps = sub // LANES
    TB = sub * n_sub

    k = fk_ref[i]
    n_fetch = fblk_ref[fblk_ref.shape[0] - 1]

    def copy(kk):
        slot = lax.rem(kk, FFN_RING)
        row0 = pl.multiple_of(fblk_ref[kk] * TB, TB)
        return pltpu.make_async_copy(x_hbm.at[pl.ds(row0, TB), :], xbuf_ref.at[slot], sem_ref.at[slot])

    @pl.when((fl & 8) != 0)
    def _():
        @pl.when(i == 0)
        def _():
            copy(0).start()

            @pl.when(n_fetch > 1)
            def _():
                copy(1).start()

        @pl.when(k + 2 < n_fetch)
        def _():
            copy(k + 2).start()

        copy(k).wait()

    x_ref = xbuf_ref.at[lax.rem(k, FFN_RING)]

    @pl.when((fl & 1) != 0)
    def _():
        xacc_ref[...] = jnp.zeros(xacc_ref.shape, F32)
        gacc_ref[...] = jnp.zeros(gacc_ref.shape, F32)

    for a in range(n_sub):
        tsub = st_ref[i] * n_sub + a
        lo = offs_ref[tsub * N_EXPERTS + e]
        hi = offs_ref[(tsub + 1) * N_EXPERTS + e]

        @pl.when(((fl & 4) != 0) & (hi > s * T) & (lo < (s + 1) * T) & (hi > lo))
        def _(a=a):
            blocks = range(a * bps, (a + 1) * bps)
            P = _one_hot_slots(rank_ref, e, s * T, T, blocks)
            xacc_ref[...] += _dot(P, x_ref[a * sub:(a + 1) * sub, :])
            g = jnp.concatenate([gate_ref[b, pl.ds(e, 1), :] for b in blocks], axis=1)
            parts = [t.astype(F32) for t in _split3(g)]
            grows = jnp.concatenate(parts + [jnp.zeros((13, sub), F32)], axis=0).astype(BF)
            gacc_ref[...] += _dot_nt(P, grows)

    @pl.when((fl & 2) != 0)
    def _():
        x = xacc_ref[...].astype(BF)
        y = jnp.zeros((T, D_MODEL), F32)
        for c in range(EXPERT_FF // f_chunk):
            sl = slice(c * f_chunk, (c + 1) * f_chunk)
            gt = _dot(x, wg_ref[:, sl])
            up = _dot(x, wu_ref[:, sl])
            hid = (gt * _sigmoid(gt) * up).astype(BF)
            y = y + _dot(hid, wd_ref[sl, :])
        ga = gacc_ref[...]
        gcol = ga[:, 0:1] + ga[:, 1:2] + ga[:, 2:3]
        ye_ref[...] = (y * gcol).astype(ye_ref.dtype)


def _ffn(sched, offs_flat, h2, rankm, gate, wg, wu, wd, cap, T, sub, n_sub):
    se, ss, st, fl = sched
    n_steps = se.shape[0]
    TB = sub * n_sub
    nb = TB // LANES
    kern = functools.partial(_ffn_kernel, T=T, sub=sub, n_sub=n_sub, f_chunk=1024)
    valid = (fl & 4) != 0
    new = valid & jnp.concatenate([jnp.ones((1,), bool), st[1:] != st[:-1]])
    fk = jnp.cumsum(new.astype(jnp.int32)) - 1
    fblk = jnp.zeros((n_steps + 2,), jnp.int32).at[jnp.where(new, fk, n_steps)].set(st)
    fblk = fblk.at[n_steps + 1].set(jnp.sum(new.astype(jnp.int32)))
    fl = fl + 8 * new.astype(jnp.int32)
    grid_spec = pltpu.PrefetchScalarGridSpec(
        num_scalar_prefetch=7,
        grid=(n_steps,),
        in_specs=[
            pl.BlockSpec(memory_space=pl.ANY),
            pl.BlockSpec((nb, N_EXPERTS, LANES), lambda i, se, ss, st, *_: (st[i], 0, 0)),
            pl.BlockSpec((nb, N_EXPERTS, LANES), lambda i, se, ss, st, *_: (st[i], 0, 0)),
            pl.BlockSpec((None, D_MODEL, EXPERT_FF), lambda i, se, *_: (se[i], 0, 0)),
            pl.BlockSpec((None, D_MODEL, EXPERT_FF), lambda i, se, *_: (se[i], 0, 0)),
            pl.BlockSpec((None, EXPERT_FF, D_MODEL), lambda i, se, *_: (se[i], 0, 0)),
        ],
        out_specs=pl.BlockSpec((None, T, D_MODEL), lambda i, se, ss, *_: (se[i], ss[i], 0)),
        scratch_shapes=[pltpu.VMEM((T, D_MODEL), F32), pltpu.VMEM((T, 16), F32),
                        pltpu.VMEM((FFN_RING, TB, D_MODEL), BF), pltpu.SemaphoreType.DMA((FFN_RING,))],
    )
    return pl.pallas_call(
        kern,
        grid_spec=grid_spec,
        out_shape=jax.ShapeDtypeStruct((N_EXPERTS, cap, D_MODEL), BF),
        compiler_params=_cp(("arbitrary",)),
        name="moe_ffn",
    )(se, ss, st, fl, offs_flat, fk, fblk, h2, rankm, gate, wg, wu, wd)


def _combine_kernel(se_ref, ss_ref, st_ref, fl_ref, offs_ref, ye_ref, rank_ref, x_ref, g2_ref, o_ref,
                    acc_ref, *, T, n_tiles, n_blocks, subs_per_tile):
    i = pl.program_id(0)
    e = se_ref[i]
    s = ss_ref[i]
    t = st_ref[i]
    fl = fl_ref[i]

    @pl.when((fl & 1) != 0)
    def _():
        acc_ref[...] = jnp.zeros(acc_ref.shape, F32)

    lo = offs_ref[t * subs_per_tile * N_EXPERTS + e]
    hi = offs_ref[(t + 1) * subs_per_tile * N_EXPERTS + e]
    for j in range(n_tiles):
        slot0 = (s * n_tiles + j) * T

        @pl.when(((fl & 4) != 0) & (hi > slot0) & (lo < slot0 + T) & (hi > lo))
        def _(j=j, slot0=slot0):
            P = _one_hot_slots(rank_ref, e, slot0, T, range(n_blocks))
            acc_ref[...] += _dot_tn(P, ye_ref[j * T:(j + 1) * T, :])

    @pl.when((fl & 2) != 0)
    def _():
        o_ref[...] = x_ref[...] + g2_ref[...] * acc_ref[...]


def _combine(sched, offs_flat, ye, rankm, x1, g2, S, T, n_tiles, TT, sub):
    se, ss, st, fl = sched
    n_steps = se.shape[0]
    N = x1.shape[0]
    nb = TT // LANES
    spb = S // TT
    YB = T * n_tiles
    kern = functools.partial(_combine_kernel, T=T, n_tiles=n_tiles, n_blocks=nb, subs_per_tile=TT // sub)
    grid_spec = pltpu.PrefetchScalarGridSpec(
        num_scalar_prefetch=5,
        grid=(n_steps,),
        in_specs=[
            pl.BlockSpec((None, YB, D_MODEL), lambda i, se, ss, st, fl, of: (se[i], ss[i], 0)),
            pl.BlockSpec((nb, N_EXPERTS, LANES), lambda i, se, ss, st, fl, of: (st[i], 0, 0)),
            pl.BlockSpec((TT, D_MODEL), lambda i, se, ss, st, fl, of: (st[i], 0)),
            pl.BlockSpec((None, 1, D_MODEL), lambda i, se, ss, st, fl, of: (st[i] // spb, 0, 0)),
        ],
        out_specs=pl.BlockSpec((TT, D_MODEL), lambda i, se, ss, st, fl, of: (st[i], 0)),
        scratch_shapes=[pltpu.VMEM((TT, D_MODEL), F32)],
    )
    return pl.pallas_call(
        kern,
        grid_spec=grid_spec,
        out_shape=jax.ShapeDtypeStruct((N, D_MODEL), F32),
        compiler_params=_cp(("arbitrary",)),
        name="moe_combine",
    )(se, ss, st, fl, offs_flat, ye, rankm, x1, g2)


MOE_T = 256
MOE_SUB = 512
FFN_SUBS = 4
COMBINE_TT = 1024
YE_TILES = 2


def _moe(h2, lg, x1, g2, S, wg, wu, wd):
    N = x1.shape[0]
    cap = (CAPACITY_FACTOR * N) // N_EXPERTS
    T = min(MOE_T, cap)
    sub = min(MOE_SUB, S)
    n_sub = min(FFN_SUBS, S // sub)
    TT = min(COMBINE_TT, S)
    n_tiles = min(YE_TILES, cap // T)
    rankm, gate, offs = _router(lg, cap, sub)
    offs_i = offs[:, :, 0].astype(jnp.int32)
    offs_flat = offs_i.reshape(-1)
    fsched = _schedule(offs_i[::n_sub], cap, T, token_major=False)
    csched = _schedule(offs_i[::TT // sub], cap, T * n_tiles, token_major=True)
    ye = _ffn(fsched, offs_flat, h2, rankm, gate, wg, wu, wd, cap, T, sub, n_sub)
    return _combine(csched, offs_flat, ye, rankm, x1, g2, S, T, n_tiles, TT, sub)


def _trunk(x, mods, P):
    B, S, _ = x.shape
    x2 = x.reshape(B * S, D_MODEL)
    for l in range(DEPTH):
        mod = mods[l]
        sh1, sc1, g1, sh2, sc2, g2 = [mod[:, j][:, None, :] for j in range(N_MOD)]
        n1 = P["norm_g"][l, 0][None, :]
        n2 = P["norm_g"][l, 1][None, :]
        j = l // 2
        if l % 2 == 0:
            lambda_init = 0.8 - 0.6 * math.exp(-0.3 * l)
            qkv = _inproj_attn(x2, S, n1, sc1, sh1, P["attn_w_in"][j], P["attn_gain"][j])
            a = _attention(qkv, B, S, P["slopes"], P["attn_dmax"][j], P["attn_lambda"][j],
                           P["attn_subln"][j][None, :], lambda_init, tq=1024 if S >= 8192 else 512,
                           group=4 if S >= 8192 else 3)
            x1, h2, lg = _outproj("attn", (a,), x2, S, g1, P["attn_w_out"][j], n2, sc2, sh2,
                                  P["w_routerT"][l])
        else:
            proj, gates = _inproj_mlstm(x2, S, n1, sc1, sh1, P["mlstm_w_main"][j], P["mlstm_w_gate"][j],
                                        P["mlstm_b_gate"][j])
            h_fw, h_bw = _mlstm(proj, gates, B, S)
            x1, h2, lg = _outproj("mlstm", (h_fw, h_bw, proj, P["mlstm_out_norm"][j]), x2, S, g1,
                                  P["mlstm_w_out"][j], n2, sc2, sh2, P["w_routerT"][l])
        x2 = _moe(h2, lg, x1, g2, S, P["w_exp_gate"][l], P["w_exp_up"][l], P["w_exp_down"][l])
    return x2.reshape(B, S, D_MODEL)


def _prepare(norm_g, attn_w_in, attn_q_gain, attn_k_gain, attn_lambda, attn_subln, attn_w_out,
             mlstm_w_in, mlstm_b_gate, mlstm_out_norm, mlstm_w_out, w_router, w_exp_gate, w_exp_up,
             w_exp_down):
    H = ATTN_HEADS
    n_main = 2 * MLSTM_HEADS * MLSTM_DK + MLSTM_HEADS * MLSTM_DV + D_MODEL
    n_gate = 4 * MLSTM_HEADS
    qg = jnp.tile(attn_q_gain * (ATTN_HEAD_DIM ** -0.5 * _LOG2E), (1, 2 * H))
    kg = jnp.tile(attn_k_gain, (1, 2 * H))
    slopes = 2.0 ** (-8.0 * np.arange(1, H + 1) / H)
    assert all(np.frexp(slopes)[0] == 0.5), "ALiBi slopes must be powers of two for bf16-exact features"
    smax = (8.0 * _LOG2E * 1.02) * jnp.max(jnp.abs(attn_q_gain), axis=1) * jnp.max(jnp.abs(attn_k_gain), axis=1)
    dmax = jnp.ceil((152.0 + 2.0 * smax[:, None]) / jnp.asarray(slopes * _LOG2E, F32)[None, :])
    dmax = jnp.clip(dmax, 0.0, 2.0 ** 30).astype(jnp.int32)
    return {
        "attn_dmax": dmax,
        "norm_g": norm_g,
        "attn_w_in": attn_w_in.astype(BF),
        "attn_gain": jnp.concatenate([qg, kg], axis=1)[:, None, :],
        "attn_lambda": attn_lambda,
        "attn_subln": attn_subln,
        "attn_w_out": attn_w_out.astype(BF),
        "slopes": jnp.asarray(slopes, F32),
        "mlstm_w_main": mlstm_w_in[:, :, :n_main].astype(BF),
        "mlstm_w_gate": jnp.pad(mlstm_w_in[:, :, n_main:], ((0, 0), (0, 0), (0, LANES - n_gate))).astype(BF),
        "mlstm_b_gate": jnp.pad(mlstm_b_gate, ((0, 0), (0, LANES - n_gate)))[:, None, :],
        "mlstm_out_norm": jnp.tile(mlstm_out_norm, (1, MLSTM_HEADS))[:, None, :],
        "mlstm_w_out": mlstm_w_out.astype(BF),
        "w_routerT": jnp.swapaxes(w_router, 1, 2).astype(BF),
        "w_exp_gate": w_exp_gate.astype(BF),
        "w_exp_up": w_exp_up.astype(BF),
        "w_exp_down": w_exp_down.astype(BF),
    }


def kernel(x_prompt, x_sample, c_prompt, c_sample, norm_g, w_ada, b_ada, attn_w_in, attn_q_gain,
           attn_k_gain, attn_lambda, attn_subln, attn_w_out, mlstm_w_in, mlstm_b_gate, mlstm_out_norm,
           mlstm_w_out, w_router, w_exp_gate, w_exp_up, w_exp_down):
    P = _prepare(norm_g, attn_w_in, attn_q_gain, attn_k_gain, attn_lambda, attn_subln, attn_w_out,
                 mlstm_w_in, mlstm_b_gate, mlstm_out_norm, mlstm_w_out, w_router, w_exp_gate,
                 w_exp_up, w_exp_down)
    bp, bs = c_prompt.shape[0], c_sample.shape[0]
    pad = (-(bp + bs)) % 8
    c_all = jnp.concatenate([c_prompt, c_sample, jnp.zeros((pad, D_MODEL), F32)], axis=0)
    mods = _ada(c_all, w_ada, b_ada)
    mods = mods.reshape(DEPTH, c_all.shape[0], N_MOD, D_MODEL)
    y_prompt = _trunk(x_prompt, mods[:, :bp], P)
    y_sample = _trunk(x_sample, mods[:, bp:bp + bs], P)
    return (y_prompt, y_sample)
```
